```python
import math
import jax, jax.numpy as jnp
from jax import lax
import numpy as np

D_MODEL = 2048
BATCH = 4
SEQ = 4096
DEPTH = 1
DEC_BATCH = 1
DEC_SEQ = 16384
PAST_LEN = 128

D_MIX = D_MODEL
GDN_HEADS = 8
GDN_HEAD_DIM = D_MODEL // 16
GDN_WIDTH = GDN_HEADS * GDN_HEAD_DIM
POOL_WIDTH = D_MIX - GDN_WIDTH
POOL_WINDOWS = (2, 4, 8, 16)
POOL_GROUP = POOL_WIDTH // len(POOL_WINDOWS)
CONV_K = 5
CHUNK = 64
IN_DIM = 4 * GDN_WIDTH + 4 * GDN_HEADS + POOL_WIDTH
PEER_HEADS = 8
N_KEYS = 128
N_EXPERTS = N_KEYS * N_KEYS
PEER_TOPK = 16
PEER_QDIM = 256
PEER_QHALF = PEER_QDIM // 2
PEER_BLOCK = 128
EPS = 1e-6

kernel_name = 'hybrid_gdn_pool_peer_encoder'


def _rms_norm(x, w):
    xf = x.astype(jnp.float32)
    y = xf * lax.rsqrt(jnp.mean(xf * xf, axis=-1, keepdims=True) + EPS)
    return (y * w.astype(jnp.float32)).astype(x.dtype)


def _l2norm(x):
    xf = x.astype(jnp.float32)
    return xf * lax.rsqrt(jnp.sum(xf * xf, axis=-1, keepdims=True) + EPS)


def _gated_delta_chunked(q, k, v, g, beta):
    B, H, T, dk = q.shape
    dv = v.shape[-1]
    n = T // CHUNK
    q = q.reshape(B, H, n, CHUNK, dk)
    k = k.reshape(B, H, n, CHUNK, dk)
    v = v.reshape(B, H, n, CHUNK, dv)
    beta = beta.reshape(B, H, n, CHUNK)
    gc = jnp.cumsum(g.reshape(B, H, n, CHUNK), axis=-1)
    tri_incl = jnp.tril(jnp.ones((CHUNK, CHUNK), dtype=bool))
    tri_strict = jnp.tril(jnp.ones((CHUNK, CHUNK), dtype=bool), -1)
    diff = gc[..., :, None] - gc[..., None, :]
    decay = jnp.exp(jnp.where(tri_incl, diff, -jnp.inf))
    kb = k * beta[..., None]
    vb = v * beta[..., None]
    a_strict = jnp.where(tri_strict, jnp.einsum('bhncd,bhnsd->bhncs', kb, k) * decay, 0.0)
    lmat = a_strict + jnp.eye(CHUNK, dtype=jnp.float32)
    u = lax.linalg.triangular_solve(lmat, vb, left_side=True, lower=True, unit_diagonal=True)
    w = lax.linalg.triangular_solve(lmat, kb * jnp.exp(gc)[..., None], left_side=True, lower=True, unit_diagonal=True)
    attn_intra = jnp.einsum('bhncd,bhnsd->bhncs', q, k) * decay
    q_dec = q * jnp.exp(gc)[..., None]
    k_dec = k * jnp.exp(gc[..., -1:] - gc)[..., None]
    g_last = jnp.exp(gc[..., -1])

    def step(state, xs):
        u_c, w_c, qd_c, kd_c, at_c, gl_c = xs
        v_new = u_c - jnp.einsum('bhcd,bhde->bhce', w_c, state)
        o_c = jnp.einsum('bhcd,bhde->bhce', qd_c, state) + jnp.einsum('bhcs,bhse->bhce', at_c, v_new)
        state = state * gl_c[..., None, None] + jnp.einsum('bhcd,bhce->bhde', kd_c, v_new)
        return state, o_c

    xs = (jnp.moveaxis(u, 2, 0), jnp.moveaxis(w, 2, 0), jnp.moveaxis(q_dec, 2, 0),
          jnp.moveaxis(k_dec, 2, 0), jnp.moveaxis(attn_intra, 2, 0), jnp.moveaxis(g_last, 2, 0))
    state0 = jnp.zeros((B, H, dk, dv), dtype=jnp.float32)
    _, o = lax.scan(step, state0, xs)
    return jnp.moveaxis(o, 0, 2).reshape(B, H, T, dv)


def _gdn_mixer(qkv, z, a, b, conv_w, a_log, dt_bias, norm_w):
    B, T, C3 = qkv.shape
    qkv = lax.conv_general_dilated(qkv, conv_w[:, None, :], window_strides=(1,),
                                   padding=[(CONV_K // 2, CONV_K // 2)],
                                   dimension_numbers=('NWC', 'WIO', 'NWC'),
                                   feature_group_count=C3)
    qkv = jax.nn.silu(qkv)
    q, k, v = jnp.split(qkv, 3, axis=-1)

    def heads(t):
        return t.reshape(B, T, GDN_HEADS, GDN_HEAD_DIM).transpose(0, 2, 1, 3)

    q = _l2norm(heads(q)) * (GDN_HEAD_DIM ** -0.5)
    k = _l2norm(heads(k))
    v = heads(v).astype(jnp.float32)
    af = a.reshape(B, T, 2, GDN_HEADS).astype(jnp.float32)
    bf = b.reshape(B, T, 2, GDN_HEADS).astype(jnp.float32)
    g = -jnp.exp(a_log.astype(jnp.float32)) * jax.nn.softplus(af + dt_bias.astype(jnp.float32))
    beta = jax.nn.sigmoid(bf)
    g = g.transpose(2, 0, 3, 1)
    beta = beta.transpose(2, 0, 3, 1)

    def flip(t):
        return jnp.flip(t, axis=2)

    o_fwd = _gated_delta_chunked(q, k, v, g[0], beta[0])
    o_bwd = flip(_gated_delta_chunked(flip(q), flip(k), flip(v), flip(g[1]), flip(beta[1])))
    o = (o_fwd + o_bwd).transpose(0, 2, 1, 3)
    o = _rms_norm(o, norm_w)
    zz = z.reshape(B, T, GDN_HEADS, GDN_HEAD_DIM).astype(jnp.float32)
    return (o * jax.nn.silu(zz)).reshape(B, T, GDN_WIDTH).astype(z.dtype)


def _pool_mixer(p, pool_w, pool_scale):
    B, T, _ = p.shape
    pg = p.reshape(B, T, len(POOL_WINDOWS), POOL_GROUP).astype(jnp.float32)
    t = jnp.arange(T)
    outs = []
    for i, win in enumerate(POOL_WINDOWS):
        half = win // 2
        xi = pg[:, :, i]
        xp = jnp.pad(xi, ((0, 0), (half, half), (0, 0)))
        cs = jnp.pad(lax.cumsum(xp, axis=1), ((0, 0), (1, 0), (0, 0)))
        wsum = cs[:, win:win + T] - cs[:, :T]
        cnt = (jnp.minimum(t + half, T) - jnp.maximum(t - half, 0)).astype(jnp.float32)
        outs.append(wsum / cnt[None, :, None] - xi)
    pooled = jnp.stack(outs, axis=2)
    mixed = jnp.einsum('btgc,gcd->btgd', pooled, pool_w.astype(jnp.float32))
    return (mixed.reshape(B, T, POOL_WIDTH) * pool_scale.astype(jnp.float32)).astype(p.dtype)


def _peer(h, peer_wq, peer_keys, expert_u, expert_v):
    B, T, D = h.shape
    q = (h @ peer_wq).reshape(B, T, PEER_HEADS, 2, PEER_QHALF).astype(jnp.float32)
    s = jnp.einsum('bthpd,hpkd->bthpk', q, peer_keys.astype(jnp.float32))
    sv, si = lax.top_k(s, PEER_TOPK)
    cand = sv[..., 0, :, None] + sv[..., 1, None, :]
    cidx = si[..., 0, :, None] * N_KEYS + si[..., 1, None, :]
    cand = cand.reshape(B, T, PEER_HEADS, PEER_TOPK * PEER_TOPK)
    cidx = cidx.reshape(B, T, PEER_HEADS, PEER_TOPK * PEER_TOPK)
    fv, fi = lax.top_k(cand, PEER_TOPK)
    eidx = jnp.take_along_axis(cidx, fi, axis=-1)
    gates = jax.nn.softmax(fv, axis=-1)
    m = B * T
    nb = m // PEER_BLOCK
    hb = h.reshape(nb, PEER_BLOCK, D)
    ib = eidx.reshape(nb, PEER_BLOCK, PEER_HEADS * PEER_TOPK)
    gb = gates.reshape(nb, PEER_BLOCK, PEER_HEADS * PEER_TOPK).astype(h.dtype)

    def block(args):
        hx, ix, gx = args
        u = jnp.take(expert_u, ix, axis=0)
        v = jnp.take(expert_v, ix, axis=0)
        act = jax.nn.gelu(jnp.einsum('td,ted->te', hx, u), approximate=False) * gx
        return jnp.einsum('te,ted->td', act, v)

    y = lax.map(block, (hb, ib, gb))
    return y.reshape(B, T, D)


def _encoder_layer(x, c, w_ada, b_ada, norm1_w, w_in, conv_w, a_log, dt_bias, gdn_norm_w,
                   pool_w, pool_scale, w_out, norm2_w, peer_wq, peer_keys, expert_u, expert_v):
    mod = jax.nn.silu(c) @ w_ada + b_ada
    shift1, scale1, gate1, shift2, scale2, gate2 = [m_[:, None, :] for m_ in jnp.split(mod, 6, axis=-1)]
    h = _rms_norm(x, norm1_w) * (1 + scale1) + shift1
    proj = h @ w_in
    o0 = 3 * GDN_WIDTH
    o1 = o0 + GDN_WIDTH
    o2 = o1 + 2 * GDN_HEADS
    o3 = o2 + 2 * GDN_HEADS
    y_a = _gdn_mixer(proj[..., :o0], proj[..., o0:o1], proj[..., o1:o2], proj[..., o2:o3],
                     conv_w, a_log, dt_bias, gdn_norm_w)
    y_b = _pool_mixer(proj[..., o3:], pool_w, pool_scale)
    mixed = jnp.concatenate([y_a, y_b], axis=-1) @ w_out
    x = x + gate1 * mixed
    h = _rms_norm(x, norm2_w) * (1 + scale2) + shift2
    x = x + gate2 * _peer(h, peer_wq, peer_keys, expert_u, expert_v)
    return x


def _trunk(x, c, w_ada, b_ada, norm1_w, w_in, conv_w, a_log, dt_bias, gdn_norm_w,
           pool_w, pool_scale, w_out, norm2_w, peer_wq, peer_keys, expert_u, expert_v, norm_f_w):
    for l in range(DEPTH):
        x = _encoder_layer(x, c, w_ada[l], b_ada[l], norm1_w[l], w_in[l], conv_w[l], a_log[l],
                           dt_bias[l], gdn_norm_w[l], pool_w[l], pool_scale[l], w_out[l],
                           norm2_w[l], peer_wq[l], peer_keys[l], expert_u[l], expert_v[l])
    return _rms_norm(x, norm_f_w)


def setup_inputs(seed: int = 0) -> dict:
    key = jax.random.key(seed)
    ks = jax.random.split(key, 24)
    f32 = jnp.float32
    nrm = jax.random.normal
    x_prompt = nrm(ks[0], (BATCH, SEQ, D_MODEL), f32)
    x_sample = nrm(ks[1], (DEC_BATCH, DEC_SEQ, D_MODEL), f32)
    c_prompt = nrm(ks[2], (BATCH, D_MODEL), f32)
    c_sample = nrm(ks[3], (DEC_BATCH, D_MODEL), f32)
    w_ada = nrm(ks[4], (DEPTH, D_MODEL, 6 * D_MODEL), f32) * (D_MODEL ** -0.5)
    b_ada = nrm(ks[5], (DEPTH, 6 * D_MODEL), f32) * 0.02
    norm1_w = 1.0 + 0.02 * nrm(ks[6], (DEPTH, D_MODEL), f32)
    w_in = nrm(ks[7], (DEPTH, D_MODEL, IN_DIM), f32) * (D_MODEL ** -0.5)
    conv_w = nrm(ks[8], (DEPTH, CONV_K, 3 * GDN_WIDTH), f32) * (CONV_K ** -0.5)
    a_log = jnp.log(jax.random.uniform(ks[9], (DEPTH, 2, GDN_HEADS), f32, minval=1.0, maxval=16.0))
    dt = jnp.exp(jax.random.uniform(ks[10], (DEPTH, 2, GDN_HEADS), f32,
                                    minval=math.log(1e-3), maxval=math.log(1e-1)))
    dt_bias = dt + jnp.log(-jnp.expm1(-dt))
    gdn_norm_w = 1.0 + 0.02 * nrm(ks[11], (DEPTH, GDN_HEAD_DIM), f32)
    pool_w = nrm(ks[12], (DEPTH, len(POOL_WINDOWS), POOL_GROUP, POOL_GROUP), f32) * (POOL_GROUP ** -0.5)
    pool_scale = 1.0 + 0.02 * nrm(ks[13], (DEPTH, POOL_WIDTH), f32)
    w_out = nrm(ks[14], (DEPTH, D_MIX, D_MODEL), f32) * (D_MIX ** -0.5)
    norm2_w = 1.0 + 0.02 * nrm(ks[15], (DEPTH, D_MODEL), f32)
    peer_wq = nrm(ks[16], (DEPTH, D_MODEL, PEER_HEADS * PEER_QDIM), f32) * (D_MODEL ** -0.5)
    peer_keys = nrm(ks[17], (DEPTH, PEER_HEADS, 2, N_KEYS, PEER_QHALF), f32) * (PEER_QHALF ** -0.5)
    expert_u = nrm(ks[18], (DEPTH, N_EXPERTS, D_MODEL), f32) * (D_MODEL ** -0.5)
    expert_v = nrm(ks[19], (DEPTH, N_EXPERTS, D_MODEL), f32) * 0.5
    norm_f_w = 1.0 + 0.02 * nrm(ks[20], (D_MODEL,), f32)
    return {'x_prompt': x_prompt, 'x_sample': x_sample, 'c_prompt': c_prompt, 'c_sample': c_sample,
            'w_ada': w_ada, 'b_ada': b_ada, 'norm1_w': norm1_w, 'w_in': w_in, 'conv_w': conv_w,
            'a_log': a_log, 'dt_bias': dt_bias, 'gdn_norm_w': gdn_norm_w, 'pool_w': pool_w,
            'pool_scale': pool_scale, 'w_out': w_out, 'norm2_w': norm2_w, 'peer_wq': peer_wq,
            'peer_keys': peer_keys, 'expert_u': expert_u, 'expert_v': expert_v, 'norm_f_w': norm_f_w}


def reference(x_prompt, x_sample, c_prompt, c_sample, w_ada, b_ada, norm1_w, w_in, conv_w, a_log,
              dt_bias, gdn_norm_w, pool_w, pool_scale, w_out, norm2_w, peer_wq, peer_keys,
              expert_u, expert_v, norm_f_w):
    y_prompt = _trunk(x_prompt, c_prompt, w_ada, b_ada, norm1_w, w_in, conv_w, a_log, dt_bias,
                      gdn_norm_w, pool_w, pool_scale, w_out, norm2_w, peer_wq, peer_keys,
                      expert_u, expert_v, norm_f_w)
    y_sample = _trunk(x_sample, c_sample, w_ada, b_ada, norm1_w, w_in, conv_w, a_log, dt_bias,
                      gdn_norm_w, pool_w, pool_scale, w_out, norm2_w, peer_wq, peer_keys,
                      expert_u, expert_v, norm_f_w)
    return (y_prompt, y_sample)
```

```python
import functools
import math

import jax
import jax.numpy as jnp
from jax import lax
from jax.experimental import pallas as pl
from jax.experimental.pallas import tpu as pltpu

F32 = jnp.float32
BF16 = jnp.bfloat16

D_MODEL = 2048
GDN_HEADS = 8
HEAD_DIM = 128
GDN_WIDTH = GDN_HEADS * HEAD_DIM
POOL_WIDTH = 1024
POOL_WINDOWS = (2, 4, 8, 16)
POOL_GROUP = 256
CONV_K = 5
CHUNK = 64
PEER_HEADS = 8
N_KEYS = 128
PEER_TOPK = 16
EPS = 1e-6

LANES = 128
HALO = 16
SUB = 256
VMEM_LIMIT = 56 * 1024 * 1024
NEG = -1e30


def _cparams(sem):
    return pltpu.CompilerParams(dimension_semantics=sem, vmem_limit_bytes=VMEM_LIMIT)


def _sigmoid(x):
    return 1.0 / (1.0 + jnp.exp(-x))


def _softplus(x):
    return jnp.maximum(x, 0.0) + jnp.log1p(jnp.exp(-jnp.abs(x)))


def _split(x):
    hi = x.astype(BF16)
    lo = (x - hi.astype(F32)).astype(BF16)
    return hi, lo


def _dot(a, b):
    return jnp.dot(a, b, preferred_element_type=F32)


def _dot3(a, b):
    ah, al = _split(a)
    bh, bl = _split(b)
    return _dot(ah, bh) + _dot(ah, bl) + _dot(al, bh)


def _dot_nt(a, b):
    return lax.dot_general(a, b, (((1,), (1,)), ((), ())), preferred_element_type=F32)


def _dot_tn(a, b):
    return lax.dot_general(a, b, (((0,), (0,)), ((), ())), preferred_element_type=F32)


def _ada_kernel(c_ref, w_ref, b_ref, o_ref):
    c = c_ref[...]
    sc = c * _sigmoid(c)
    o_ref[...] = jnp.dot(sc, w_ref[...], preferred_element_type=F32,
                         precision=lax.Precision.HIGHEST) + b_ref[...]


def _ada(c8, w_ada, b_ada):
    n = w_ada.shape[1]
    tn = 1024
    return pl.pallas_call(
        _ada_kernel,
        grid=(n // tn,),
        in_specs=[pl.BlockSpec((8, D_MODEL), lambda j: (0, 0)),
                  pl.BlockSpec((D_MODEL, tn), lambda j: (0, j)),
                  pl.BlockSpec((1, tn), lambda j: (0, j))],
        out_specs=pl.BlockSpec((8, tn), lambda j: (0, j)),
        out_shape=jax.ShapeDtypeStruct((8, n), F32),
        compiler_params=_cparams(("arbitrary",)),
        name="ada",
    )(c8, w_ada, b_ada.reshape(1, n))


def _chunk_cumsum(g, axis, reverse):
    n = g.shape[axis]
    pos = lax.broadcasted_iota(jnp.int32, g.shape, axis) % CHUNK
    acc = g
    s = 1
    while s < CHUNK:
        if reverse:
            sh = pltpu.roll(acc, n - s, axis)
            acc = acc + jnp.where(pos < CHUNK - s, sh, 0.0)
        else:
            sh = pltpu.roll(acc, s, axis)
            acc = acc + jnp.where(pos >= s, sh, 0.0)
        s *= 2
    return acc


def _in_kernel(x_ref, mod_ref, n1_ref, wmain_ref, wab_ref, alog_ref, dtb_ref,
               proj_ref, colg_ref, rowg_ref, h_scr):
    j = pl.program_id(1)

    @pl.when(j == 0)
    def _():
        x = x_ref[...]
        ms = jnp.mean(x * x, axis=-1, keepdims=True)
        y = x * lax.rsqrt(ms + EPS) * n1_ref[...]
        h = y * (1.0 + mod_ref[0, 1:2, :]) + mod_ref[0, 0:1, :]
        hb = h.astype(BF16)
        h_scr[...] = hb
        ab = _dot(hb, wab_ref[...])
        g = -jnp.exp(alog_ref[...]) * _softplus(ab + dtb_ref[...])
        beta = _sigmoid(ab)
        cum_f = _chunk_cumsum(g, 0, False)
        cum_b = _chunk_cumsum(g, 0, True)
        lane = lax.broadcasted_iota(jnp.int32, g.shape, 1)
        gc = jnp.where(lane % 16 < GDN_HEADS, cum_f, cum_b)
        gtot = cum_f + cum_b - g
        colg = jnp.where(lane < 16, gc, jnp.where(lane < 32, beta, jnp.where(lane < 48, gtot, 0.0)))
        colg_ref[...] = colg
        rowg_ref[...] = colg.T[0:16, :]

    proj_ref[...] = _dot(h_scr[...], wmain_ref[...]).astype(BF16)


def _stage_in(x2d, mod3, mod_base, t_len, n1, wmain, wab, alog, dtb, tm):
    m = x2d.shape[0]
    nt = m // tm
    nj = wmain.shape[1] // 1024
    tps = t_len // tm
    return pl.pallas_call(
        _in_kernel,
        grid=(nt, nj),
        in_specs=[pl.BlockSpec((tm, D_MODEL), lambda i, j: (i, 0)),
                  pl.BlockSpec((1, 6, D_MODEL), lambda i, j: (mod_base + i // tps, 0, 0)),
                  pl.BlockSpec((1, D_MODEL), lambda i, j: (0, 0)),
                  pl.BlockSpec((D_MODEL, 1024), lambda i, j: (0, j)),
                  pl.BlockSpec((D_MODEL, LANES), lambda i, j: (0, 0)),
                  pl.BlockSpec((1, LANES), lambda i, j: (0, 0)),
                  pl.BlockSpec((1, LANES), lambda i, j: (0, 0))],
        out_specs=[pl.BlockSpec((tm, 1024), lambda i, j: (i, j)),
                   pl.BlockSpec((tm, LANES), lambda i, j: (i, 0)),
                   pl.BlockSpec((16, tm), lambda i, j: (0, i))],
        out_shape=[jax.ShapeDtypeStruct((m, nj * 1024), BF16),
                   jax.ShapeDtypeStruct((m, LANES), F32),
                   jax.ShapeDtypeStruct((16, m), F32)],
        scratch_shapes=[pltpu.VMEM((tm, D_MODEL), BF16)],
        compiler_params=_cparams(("parallel", "arbitrary")),
        name="in_proj",
    )(x2d, mod3, n1, wmain, wab, alog, dtb)


def _conv_kernel(main_ref, prev_ref, next_ref, w_ref, out_ref, *, tc):
    t = pl.program_id(1)
    nt = pl.num_programs(1)
    part = pl.program_id(2)
    prev = jnp.where(t > 0, prev_ref[...].astype(F32), 0.0)
    nxt = jnp.where(t < nt - 1, next_ref[...].astype(F32), 0.0)
    xe = jnp.concatenate([prev, main_ref[...].astype(F32), nxt], axis=0)
    n = tc + 2 * HALO
    w = w_ref[...]
    acc = None
    for k in range(CONV_K):
        shift = (CONV_K // 2 - k) % n
        xs = xe if shift == 0 else pltpu.roll(xe, shift, 0)
        term = w[k:k + 1, :] * xs[HALO:HALO + tc, :]
        acc = term if acc is None else acc + term
    y = acc * _sigmoid(acc)
    outs = []
    for h in range(GDN_HEADS):
        yh = y[:, h * HEAD_DIM:(h + 1) * HEAD_DIM]
        ss = jnp.sum(yh * yh, axis=-1, keepdims=True)
        outs.append(yh * lax.rsqrt(ss + EPS))
    normed = jnp.concatenate(outs, axis=1)
    scale = jnp.where(part == 0, HEAD_DIM ** -0.5, 1.0)
    out_ref[...] = jnp.where(part < 2, normed * scale, y).astype(BF16)


def _stage_conv(proj, conv_w, nb, t_len, tc):
    m = proj.shape[0]
    nt = t_len // tc
    hb = tc // HALO
    last_halo = m // HALO - 1

    def main_map(b, t, p):
        return (b * nt + t, p)

    def prev_map(b, t, p):
        return (jnp.maximum((b * nt + t) * hb - 1, 0), p)

    def next_map(b, t, p):
        return (jnp.minimum((b * nt + t + 1) * hb, last_halo), p)

    return pl.pallas_call(
        functools.partial(_conv_kernel, tc=tc),
        grid=(nb, nt, 3),
        in_specs=[pl.BlockSpec((tc, GDN_WIDTH), main_map),
                  pl.BlockSpec((HALO, GDN_WIDTH), prev_map),
                  pl.BlockSpec((HALO, GDN_WIDTH), next_map),
                  pl.BlockSpec((CONV_K, GDN_WIDTH), lambda b, t, p: (0, p))],
        out_specs=pl.BlockSpec((tc, GDN_WIDTH), main_map),
        out_shape=jax.ShapeDtypeStruct((m, 3 * GDN_WIDTH), BF16),
        compiler_params=_cparams(("parallel", "parallel", "parallel")),
        name="conv_qkv",
    )(proj, proj, proj, conv_w)


def _tri_inv(a, ri, ci):
    b16 = (ri // 16) == (ci // 16)
    b32 = (ri // 32) == (ci // 32)
    n16 = jnp.where(b16, -a, 0.0)
    t = jnp.where(ri == ci, 1.0, 0.0) + n16
    p = n16
    for _ in range(3):
        p = _dot3(p, p)
        t = t + _dot3(t, p)
    a32 = jnp.where(jnp.logical_and(b32, jnp.logical_not(b16)), a, 0.0)
    t = t - _dot3(_dot3(t, a32), t)
    a64 = jnp.where(b32, 0.0, a)
    t = t - _dot3(_dot3(t, a64), t)
    return t


def _gdn_direction(d, h, q_ref, k_ref, v_ref, cg_ref, rg_ref, o_ref, s_scr):
    jg = d * GDN_HEADS + h
    rowi = lax.broadcasted_iota(jnp.int32, (LANES, 3 * LANES), 0)
    coli = lax.broadcasted_iota(jnp.int32, (LANES, 3 * LANES), 1)
    sel = jnp.where(rowi == jg + 16 * (coli // LANES), 1.0, 0.0).astype(BF16)
    cg = cg_ref[...]
    c_hi, c_lo = _split(cg)
    c_lo2 = (cg - c_hi.astype(F32) - c_lo.astype(F32)).astype(BF16)
    bc = _dot(c_hi, sel) + _dot(c_lo, sel) + _dot(c_lo2, sel)
    gcb = bc[:, 0:LANES]
    bb = bc[:, LANES:2 * LANES]
    gt = bc[:, 2 * LANES:3 * LANES]
    grow = rg_ref[pl.ds(jg, 1), :]

    q = q_ref[...].astype(F32)
    k = k_ref[...].astype(F32)
    v = v_ref[...].astype(F32)
    eg = jnp.exp(gcb)
    kbeta = k * bb
    vbeta = v * bb
    kbg = kbeta * eg
    qd = q * eg
    kd = k * jnp.exp(gt - gcb)

    ri = lax.broadcasted_iota(jnp.int32, (SUB, SUB), 0)
    ci = lax.broadcasted_iota(jnp.int32, (SUB, SUB), 1)
    same = (ri // CHUNK) == (ci // CHUNK)
    if d == 0:
        incl = jnp.logical_and(same, ri >= ci)
    else:
        incl = jnp.logical_and(same, ri <= ci)
    gdiff = jnp.concatenate([gcb, gcb], axis=1) - grow
    dec = jnp.exp(jnp.where(incl, gdiff, NEG))
    kb16 = k.astype(BF16)
    kq = _dot_nt(jnp.concatenate([kbeta, q], axis=0).astype(BF16), kb16)
    a = jnp.where(ri == ci, 0.0, kq[0:SUB] * dec)
    attn = kq[SUB:2 * SUB] * dec
    t = _tri_inv(a, ri, ci)
    t_hi, t_lo = _split(t)
    rhs = jnp.concatenate([vbeta, kbg], axis=1).astype(BF16)
    uw = _dot(t_hi, rhs) + _dot(t_lo, rhs)
    u = uw[:, 0:LANES]
    w = uw[:, LANES:2 * LANES]

    s = s_scr[d]
    n_c = SUB // CHUNK
    order = range(n_c) if d == 0 else range(n_c - 1, -1, -1)
    vns = [None] * n_c
    o_s = [None] * n_c
    for c in order:
        lo, hi = c * CHUNK, (c + 1) * CHUNK
        sb = s.astype(BF16)
        wq = _dot(jnp.concatenate([w[lo:hi], qd[lo:hi]], axis=0).astype(BF16), sb)
        vn = (u[lo:hi] - wq[0:CHUNK]).astype(BF16)
        vns[c] = vn
        o_s[c] = wq[CHUNK:2 * CHUNK]
        s = s * jnp.exp(gt[lo:lo + 1, :]) + _dot_tn(kd[lo:hi].astype(BF16), vn)
    s_scr[d] = s
    vn_all = jnp.concatenate(vns, axis=0)
    o = jnp.concatenate(o_s, axis=0) + _dot(attn.astype(BF16), vn_all)
    o_ref[...] = o.astype(BF16)


def _gdn_kernel(qf, kf, vf, cgf, rgf, qb, kb, vb, cgb, rgb, of_ref, ob_ref, s_scr):
    h = pl.program_id(1)

    @pl.when(pl.program_id(2) == 0)
    def _():
        s_scr[...] = jnp.zeros_like(s_scr)

    _gdn_direction(0, h, qf, kf, vf, cgf, rgf, of_ref, s_scr)
    _gdn_direction(1, h, qb, kb, vb, cgb, rgb, ob_ref, s_scr)


def _stage_gdn(qkvn, colg, rowg, nb, t_len):
    m = qkvn.shape[0]
    ns = t_len // SUB

    def fwd(b, h, t):
        return b * ns + t

    def bwd(b, h, t):
        return b * ns + (ns - 1 - t)

    def specs(blk):
        return [pl.BlockSpec((SUB, HEAD_DIM), lambda b, h, t: (blk(b, h, t), h)),
                pl.BlockSpec((SUB, HEAD_DIM), lambda b, h, t: (blk(b, h, t), GDN_HEADS + h)),
                pl.BlockSpec((SUB, HEAD_DIM), lambda b, h, t: (blk(b, h, t), 2 * GDN_HEADS + h)),
                pl.BlockSpec((SUB, LANES), lambda b, h, t: (blk(b, h, t), 0)),
                pl.BlockSpec((16, SUB), lambda b, h, t: (0, blk(b, h, t)))]

    out_f = pl.BlockSpec((SUB, HEAD_DIM), lambda b, h, t: (fwd(b, h, t), h))
    out_b = pl.BlockSpec((SUB, HEAD_DIM), lambda b, h, t: (bwd(b, h, t), h))
    return pl.pallas_call(
        _gdn_kernel,
        grid=(nb, GDN_HEADS, ns),
        in_specs=specs(fwd) + specs(bwd),
        out_specs=[out_f, out_b],
        out_shape=[jax.ShapeDtypeStruct((m, GDN_WIDTH), BF16),
                   jax.ShapeDtypeStruct((m, GDN_WIDTH), BF16)],
        scratch_shapes=[pltpu.VMEM((2, HEAD_DIM, HEAD_DIM), F32)],
        compiler_params=_cparams(("parallel", "parallel", "arbitrary")),
        name="gdn",
    )(qkvn, qkvn, qkvn, colg, rowg, qkvn, qkvn, qkvn, colg, rowg)


def _out_kernel(of_ref, ob_ref, z_ref, p_ref, pp_ref, pn_ref, band_ref, x_ref, mod_ref,
                gnw_ref, pw_ref, ps_ref, woa_ref, wob_ref, n2_ref, x1_ref, h2t_ref,
                *, tm, tps, t_len):
    i = pl.program_id(0)
    tin = i % tps
    o = of_ref[...].astype(F32) + ob_ref[...].astype(F32)
    z = z_ref[...].astype(F32)
    gnw = gnw_ref[...]
    ya = []
    for h in range(GDN_HEADS):
        oh = o[:, h * HEAD_DIM:(h + 1) * HEAD_DIM]
        ms = jnp.mean(oh * oh, axis=-1, keepdims=True)
        ya.append(oh * lax.rsqrt(ms + EPS) * gnw)
    y_a = jnp.concatenate(ya, axis=1) * (z * _sigmoid(z))

    zero = jnp.zeros((HALO, POOL_WIDTH), BF16)
    prev = jnp.where(tin > 0, pp_ref[...], zero)
    nxt = jnp.where(tin < tps - 1, pn_ref[...], zero)
    p = p_ref[...]
    pe = jnp.concatenate([prev, p, nxt], axis=0)
    tpos = tin * tm + lax.broadcasted_iota(jnp.int32, (tm, POOL_GROUP), 0)
    yb = []
    for gi, win in enumerate(POOL_WINDOWS):
        half = win // 2
        cols = slice(gi * POOL_GROUP, (gi + 1) * POOL_GROUP)
        wsum = _dot(band_ref[gi], pe[:, cols])
        cnt = (jnp.minimum(tpos + half, t_len) - jnp.maximum(tpos - half, 0)).astype(F32)
        pooled = wsum / cnt - p[:, cols].astype(F32)
        yb.append(_dot(pooled.astype(BF16), pw_ref[gi]))
    y_b = jnp.concatenate(yb, axis=1) * ps_ref[...]

    mixed = _dot(y_a.astype(BF16), woa_ref[...]) + _dot(y_b.astype(BF16), wob_ref[...])
    x1 = x_ref[...] + mod_ref[0, 2:3, :] * mixed
    x1_ref[...] = x1
    ms = jnp.mean(x1 * x1, axis=-1, keepdims=True)
    h2 = x1 * lax.rsqrt(ms + EPS) * n2_ref[...]
    h2 = h2 * (1.0 + mod_ref[0, 4:5, :]) + mod_ref[0, 3:4, :]
    h2t_ref[...] = h2.T.astype(BF16)


def _pool_bands(tm):
    t = jnp.arange(tm)[:, None]
    s = jnp.arange(tm + 2 * HALO)[None, :] - HALO
    bands = []
    for win in POOL_WINDOWS:
        half = win // 2
        bands.append(jnp.logical_and(s >= t - half, s <= t + half - 1))
    return jnp.stack(bands).astype(BF16)


def _stage_out(o_f, o_b, proj, x2d, mod3, mod_base, t_len, gnw, pool_w, pool_scale,
               wo_a, wo_b, n2, tm):
    m = x2d.shape[0]
    nt = m // tm
    tps = t_len // tm
    hb = tm // HALO
    last_halo = m // HALO - 1
    pcol = 4 * GDN_WIDTH // POOL_WIDTH
    band = _pool_bands(tm)
    const = lambda i: (0, 0)
    return pl.pallas_call(
        functools.partial(_out_kernel, tm=tm, tps=tps, t_len=t_len),
        grid=(nt,),
        in_specs=[pl.BlockSpec((tm, GDN_WIDTH), lambda i: (i, 0)),
                  pl.BlockSpec((tm, GDN_WIDTH), lambda i: (i, 0)),
                  pl.BlockSpec((tm, GDN_WIDTH), lambda i: (i, 3)),
                  pl.BlockSpec((tm, POOL_WIDTH), lambda i: (i, pcol)),
                  pl.BlockSpec((HALO, POOL_WIDTH), lambda i: (jnp.maximum(i * hb - 1, 0), pcol)),
                  pl.BlockSpec((HALO, POOL_WIDTH), lambda i: (jnp.minimum((i + 1) * hb, last_halo), pcol)),
                  pl.BlockSpec((4, tm, tm + 2 * HALO), lambda i: (0, 0, 0)),
                  pl.BlockSpec((tm, D_MODEL), lambda i: (i, 0)),
                  pl.BlockSpec((1, 6, D_MODEL), lambda i: (mod_base + i // tps, 0, 0)),
                  pl.BlockSpec((1, HEAD_DIM), const),
                  pl.BlockSpec((4, POOL_GROUP, POOL_GROUP), lambda i: (0, 0, 0)),
                  pl.BlockSpec((1, POOL_WIDTH), const),
                  pl.BlockSpec((GDN_WIDTH, D_MODEL), const),
                  pl.BlockSpec((POOL_WIDTH, D_MODEL), const),
                  pl.BlockSpec((1, D_MODEL), const)],
        out_specs=[pl.BlockSpec((tm, D_MODEL), lambda i: (i, 0)),
                   pl.BlockSpec((D_MODEL, tm), lambda i: (0, i))],
        out_shape=[jax.ShapeDtypeStruct((m, D_MODEL), F32),
                   jax.ShapeDtypeStruct((D_MODEL, m), BF16)],
        compiler_params=_cparams(("parallel",)),
        name="mix_out",
    )(o_f, o_b, proj, proj, proj, proj, band, x2d, mod3, gnw, pool_w, pool_scale, wo_a, wo_b, n2)


def _top_rows(cur, n, scr):
    for r in range(n):
        mx = jnp.max(cur, axis=0, keepdims=True)
        scr[r:r + 1, :] = mx
        cur = jnp.where(cur >= mx, NEG, cur)


def _route_kernel(h2t_ref, wqt_ref, khi_ref, klo_ref, lim_ref, r2_ref, w_ref, e2_ref,
                  qt_scr, a_scr, b_scr, c_scr):
    qt_scr[...] = _dot(wqt_ref[...], h2t_ref[...])
    for h in range(PEER_HEADS):
        sc = []
        for p in range(2):
            hp = 2 * h + p
            qh, ql = _split(qt_scr[hp * N_KEYS:(hp + 1) * N_KEYS, :])
            sc.append(_dot(khi_ref[hp], qh) + _dot(khi_ref[hp], ql) + _dot(klo_ref[hp], qh))
        s1, s2 = sc
        _top_rows(s1, PEER_TOPK, a_scr)
        _top_rows(s2, PEER_TOPK, b_scr)
        a = a_scr[...]
        b = b_scr[...]
        cands = [a[0:1, :] + b[0:8, :], a[0:1, :] + b[8:16, :]]
        for p in range(1, 8):
            cands.append(a[p:p + 1, :] + b[0:8, :])
        cands.append(a[8:16, :] + b[0:1, :])
        _top_rows(jnp.concatenate(cands, axis=0), PEER_TOPK + 1, c_scr)
        c = c_scr[0:PEER_TOPK + 1, :]
        zsum = jnp.sum(jnp.exp(c[0:PEER_TOPK, :] - c[0:1, :]), axis=0, keepdims=True)
        tau = 0.5 * (c[PEER_TOPK - 1:PEER_TOPK, :] + c[PEER_TOPK:PEER_TOPK + 1, :])
        lim = jnp.zeros_like(s1)
        r2 = jnp.zeros_like(s2)
        for qi in range(PEER_TOPK):
            bq = b[qi:qi + 1, :]
            lim = lim + jnp.where(s1 + bq >= tau, 1.0, 0.0)
            r2 = r2 + jnp.where(s2 < bq, 1.0, 0.0)
        lim_ref[h] = jnp.where(s1 >= a[PEER_TOPK - 1:PEER_TOPK, :], lim, 0.0)
        r2_ref[h] = r2
        w_ref[h] = jnp.exp(s1 - a[0:1, :]) / zsum
        e2_ref[h] = jnp.exp(s2 - b[0:1, :])


def _stage_route(h2t, wqt, k_hi, k_lo, tm):
    m = h2t.shape[1]
    nt = m // tm
    rspec = pl.BlockSpec((PEER_HEADS, N_KEYS, tm), lambda i: (0, 0, i))
    rshape = jax.ShapeDtypeStruct((PEER_HEADS, N_KEYS, m), F32)
    return pl.pallas_call(
        _route_kernel,
        grid=(nt,),
        in_specs=[pl.BlockSpec((D_MODEL, tm), lambda i: (0, i)),
                  pl.BlockSpec((D_MODEL, D_MODEL), lambda i: (0, 0)),
                  pl.BlockSpec((2 * PEER_HEADS, N_KEYS, N_KEYS), lambda i: (0, 0, 0)),
                  pl.BlockSpec((2 * PEER_HEADS, N_KEYS, N_KEYS), lambda i: (0, 0, 0))],
        out_specs=[rspec, rspec, rspec, rspec],
        out_shape=[rshape, rshape, rshape, rshape],
        scratch_shapes=[pltpu.VMEM((D_MODEL, tm), F32),
                        pltpu.VMEM((PEER_TOPK, tm), F32),
                        pltpu.VMEM((PEER_TOPK, tm), F32),
                        pltpu.VMEM((24, tm), F32)],
        compiler_params=_cparams(("parallel",)),
        name="peer_route",
    )(h2t, wqt, k_hi, k_lo)


def _peer_kernel(h2t_ref, u_ref, vt_ref, lim_ref, r2_ref, w_ref, e2_ref, x1_ref, mod_ref,
                 nf_ref, y_ref, acc_scr, act_scr, *, te):
    e = pl.program_id(1)

    @pl.when(e == 0)
    def _():
        acc_scr[...] = jnp.zeros_like(acc_scr)

    hu = _dot(u_ref[...], h2t_ref[...])
    n_sub = te // N_KEYS
    for ii in range(n_sub):
        i = e * n_sub + ii
        gate = None
        for h in range(PEER_HEADS):
            lim_row = lim_ref[h, pl.ds(i, 1), :]
            w_row = w_ref[h, pl.ds(i, 1), :]
            term = jnp.where(r2_ref[h] < lim_row, e2_ref[h] * w_row, 0.0)
            gate = term if gate is None else gate + term
        x = hu[ii * N_KEYS:(ii + 1) * N_KEYS, :]
        act = 0.5 * x * (1.0 + lax.erf(x * (2.0 ** -0.5))) * gate
        act_scr[ii * N_KEYS:(ii + 1) * N_KEYS, :] = act.astype(BF16)
    acc_scr[...] += _dot(vt_ref[...], act_scr[...])

    @pl.when(e == pl.num_programs(1) - 1)
    def _():
        xo = x1_ref[...] + mod_ref[0, 5:6, :] * acc_scr[...].T
        ms = jnp.mean(xo * xo, axis=-1, keepdims=True)
        y_ref[...] = xo * lax.rsqrt(ms + EPS) * nf_ref[...]


def _stage_peer(h2t, u_bf, vt_bf, route, x1, mod3, mod_base, t_len, nf, tm, te):
    m = x1.shape[0]
    nt = m // tm
    ne = u_bf.shape[0] // te
    tps = t_len // tm
    rspec = pl.BlockSpec((PEER_HEADS, N_KEYS, tm), lambda i, e: (0, 0, i))
    return pl.pallas_call(
        functools.partial(_peer_kernel, te=te),
        grid=(nt, ne),
        in_specs=[pl.BlockSpec((D_MODEL, tm), lambda i, e: (0, i)),
                  pl.BlockSpec((te, D_MODEL), lambda i, e: (e, 0)),
                  pl.BlockSpec((D_MODEL, te), lambda i, e: (0, e)),
                  rspec, rspec, rspec, rspec,
                  pl.BlockSpec((tm, D_MODEL), lambda i, e: (i, 0)),
                  pl.BlockSpec((1, 6, D_MODEL), lambda i, e: (mod_base + i // tps, 0, 0)),
                  pl.BlockSpec((1, D_MODEL), lambda i, e: (0, 0))],
        out_specs=pl.BlockSpec((tm, D_MODEL), lambda i, e: (i, 0)),
        out_shape=jax.ShapeDtypeStruct((m, D_MODEL), F32),
        scratch_shapes=[pltpu.VMEM((D_MODEL, tm), F32),
                        pltpu.VMEM((te, tm), BF16)],
        compiler_params=_cparams(("parallel", "arbitrary")),
        name="peer_dense",
    )(h2t, u_bf, vt_bf, *route, x1, mod3, nf)


def _prepare_weights(norm1_w, w_in, conv_w, a_log, dt_bias, gdn_norm_w, pool_w, pool_scale,
                     w_out, norm2_w, peer_wq, peer_keys, expert_u, expert_v, norm_f_w):
    o0 = 3 * GDN_WIDTH
    o1 = o0 + GDN_WIDTH
    o2 = o1 + 2 * GDN_HEADS
    o3 = o2 + 2 * GDN_HEADS
    w_in = w_in[0]
    wmain = jnp.concatenate([w_in[:, :o1], w_in[:, o3:]], axis=1).astype(BF16)
    wa, wb = w_in[:, o1:o2], w_in[:, o2:o3]
    pad = jnp.zeros((D_MODEL, LANES - 48), F32)
    wab = jnp.concatenate([wa, wb, wa, pad], axis=1).astype(BF16)
    z16 = jnp.zeros((16,), F32)
    z80 = jnp.zeros((LANES - 48,), F32)
    al = a_log[0].reshape(16)
    dtb = dt_bias[0].reshape(16)
    alog = jnp.concatenate([al, z16, al, z80]).reshape(1, LANES)
    dtbias = jnp.concatenate([dtb, z16, dtb, z80]).reshape(1, LANES)
    keys = peer_keys[0].reshape(2 * PEER_HEADS, N_KEYS, N_KEYS)
    k_hi = keys.astype(BF16)
    k_lo = (keys - k_hi.astype(F32)).astype(BF16)
    return dict(
        n1=norm1_w[0].reshape(1, D_MODEL), wmain=wmain, wab=wab, alog=alog, dtb=dtbias,
        conv_w=conv_w[0], gnw=gdn_norm_w[0].reshape(1, HEAD_DIM),
        pool_w=pool_w[0].astype(BF16), pool_scale=pool_scale[0].reshape(1, POOL_WIDTH),
        wo_a=w_out[0][:GDN_WIDTH].astype(BF16), wo_b=w_out[0][GDN_WIDTH:].astype(BF16),
        n2=norm2_w[0].reshape(1, D_MODEL), wqt=peer_wq[0].astype(BF16).T, k_hi=k_hi, k_lo=k_lo,
        u=expert_u[0].astype(BF16), vt=expert_v[0].astype(BF16).T,
        nf=norm_f_w.reshape(1, D_MODEL))


def _trunk_group(x, mod3, mod_base, wts):
    nb, t_len, _ = x.shape
    m = nb * t_len
    x2d = x.reshape(m, D_MODEL)
    proj, colg, rowg = _stage_in(x2d, mod3, mod_base, t_len, wts["n1"], wts["wmain"],
                                 wts["wab"], wts["alog"], wts["dtb"], 512)
    qkvn = _stage_conv(proj, wts["conv_w"], nb, t_len, 512)
    o_f, o_b = _stage_gdn(qkvn, colg, rowg, nb, t_len)
    x1, h2t = _stage_out(o_f, o_b, proj, x2d, mod3, mod_base, t_len, wts["gnw"], wts["pool_w"],
                         wts["pool_scale"], wts["wo_a"], wts["wo_b"], wts["n2"], 512)
    route = _stage_route(h2t, wts["wqt"], wts["k_hi"], wts["k_lo"], 512)
    y = _stage_peer(h2t, wts["u"], wts["vt"], route, x1, mod3, mod_base, t_len, wts["nf"], 512, 512)
    return y.reshape(nb, t_len, D_MODEL)


def kernel(x_prompt, x_sample, c_prompt, c_sample, w_ada, b_ada, norm1_w, w_in, conv_w, a_log,
           dt_bias, gdn_norm_w, pool_w, pool_scale, w_out, norm2_w, peer_wq, peer_keys,
           expert_u, expert_v, norm_f_w):
    nbp = c_prompt.shape[0]
    nbs = c_sample.shape[0]
    c8 = jnp.concatenate([c_prompt, c_sample, jnp.zeros((8 - nbp - nbs, D_MODEL), F32)], axis=0)
    mod3 = _ada(c8, w_ada[0], b_ada[0]).reshape(8, 6, D_MODEL)
    wts = _prepare_weights(norm1_w, w_in, conv_w, a_log, dt_bias, gdn_norm_w, pool_w, pool_scale,
                           w_out, norm2_w, peer_wq, peer_keys, expert_u, expert_v, norm_f_w)
    y_prompt = _trunk_group(x_prompt, mod3, 0, wts)
    y_sample = _trunk_group(x_sample, mod3, nbp, wts)
    return (y_prompt, y_sample)
```

```python
import functools

import jax
import jax.numpy as jnp
from jax import lax
from jax.experimental import pallas as pl
from jax.experimental.pallas import tpu as pltpu

F32 = jnp.float32
BF16 = jnp.bfloat16

D_MODEL = 2048
GDN_HEADS = 8
HEAD_DIM = 128
GDN_WIDTH = GDN_HEADS * HEAD_DIM
POOL_WIDTH = 1024
POOL_WINDOWS = (2, 4, 8, 16)
POOL_GROUP = 256
CONV_K = 5
CHUNK = 64
PEER_HEADS = 8
N_KEYS = 128
PEER_TOPK = 16
EPS = 1e-6

LANES = 128
HALO = 16
SUB = 256
GDN_HP = 2
TOKEN_TILE = 512
PEER_TE = 512
PEER_RC = 32
VMEM_LIMIT = 56 * 1024 * 1024
NEG = -1e30


def _cparams(sem):
    return pltpu.CompilerParams(dimension_semantics=sem, vmem_limit_bytes=VMEM_LIMIT)


def _sigmoid(x):
    return 1.0 / (1.0 + jnp.exp(-x))


def _softplus(x):
    return jnp.maximum(x, 0.0) + jnp.log1p(jnp.exp(-jnp.abs(x)))


def _split(x):
    hi = x.astype(BF16)
    lo = (x - hi.astype(F32)).astype(BF16)
    return hi, lo


def _dot(a, b):
    return jnp.dot(a, b, preferred_element_type=F32)


def _dot_nt(a, b):
    return lax.dot_general(a, b, (((1,), (1,)), ((), ())), preferred_element_type=F32)


def _dot_tn(a, b):
    return lax.dot_general(a, b, (((0,), (0,)), ((), ())), preferred_element_type=F32)


def _ada_kernel(c_ref, w_ref, b_ref, o_ref):
    c = c_ref[...]
    sc = c * _sigmoid(c)
    o_ref[...] = jnp.dot(sc, w_ref[...], preferred_element_type=F32,
                         precision=lax.Precision.HIGHEST) + b_ref[...]


def _ada(c8, w_ada, b_ada):
    n = w_ada.shape[1]
    tn = 1024
    return pl.pallas_call(
        _ada_kernel,
        grid=(n // tn,),
        in_specs=[pl.BlockSpec((8, D_MODEL), lambda j: (0, 0)),
                  pl.BlockSpec((D_MODEL, tn), lambda j: (0, j)),
                  pl.BlockSpec((1, tn), lambda j: (0, j))],
        out_specs=pl.BlockSpec((8, tn), lambda j: (0, j)),
        out_shape=jax.ShapeDtypeStruct((8, n), F32),
        compiler_params=_cparams(("arbitrary",)),
        name="ada",
    )(c8, w_ada, b_ada.reshape(1, n))


def _chunk_cumsum(g, axis, reverse):
    n = g.shape[axis]
    pos = lax.broadcasted_iota(jnp.int32, g.shape, axis) % CHUNK
    acc = g
    s = 1
    while s < CHUNK:
        if reverse:
            sh = pltpu.roll(acc, n - s, axis)
            acc = acc + jnp.where(pos < CHUNK - s, sh, 0.0)
        else:
            sh = pltpu.roll(acc, s, axis)
            acc = acc + jnp.where(pos >= s, sh, 0.0)
        s *= 2
    return acc


def _in_kernel(x_ref, mod_ref, n1_ref, wmain_ref, wab_ref, alog_ref, dtb_ref,
               proj_ref, colg_ref, rowg_ref, h_scr):
    j = pl.program_id(1)

    @pl.when(j == 0)
    def _():
        x = x_ref[...]
        ms = jnp.mean(x * x, axis=-1, keepdims=True)
        y = x * lax.rsqrt(ms + EPS) * n1_ref[...]
        h = y * (1.0 + mod_ref[0, 1:2, :]) + mod_ref[0, 0:1, :]
        hb = h.astype(BF16)
        h_scr[...] = hb
        ab = _dot(hb, wab_ref[...])
        g = -jnp.exp(alog_ref[...]) * _softplus(ab + dtb_ref[...])
        beta = _sigmoid(ab)
        cum_f = _chunk_cumsum(g, 0, False)
        cum_b = _chunk_cumsum(g, 0, True)
        lane = lax.broadcasted_iota(jnp.int32, g.shape, 1)
        gc = jnp.where(lane % 16 < GDN_HEADS, cum_f, cum_b)
        gtot = cum_f + cum_b - g
        colg = jnp.where(lane < 16, gc, jnp.where(lane < 32, beta, jnp.where(lane < 48, gtot, 0.0)))
        colg_ref[...] = colg
        rowg_ref[...] = colg.T[0:16, :]

    proj_ref[...] = _dot(h_scr[...], wmain_ref[...]).astype(BF16)


def _stage_in(x2d, mod3, mod_base, t_len, n1, wmain, wab, alog, dtb, tm):
    m = x2d.shape[0]
    nt = m // tm
    nj = wmain.shape[1] // 1024
    tps = t_len // tm
    return pl.pallas_call(
        _in_kernel,
        grid=(nt, nj),
        in_specs=[pl.BlockSpec((tm, D_MODEL), lambda i, j: (i, 0)),
                  pl.BlockSpec((1, 6, D_MODEL), lambda i, j: (mod_base + i // tps, 0, 0)),
                  pl.BlockSpec((1, D_MODEL), lambda i, j: (0, 0)),
                  pl.BlockSpec((D_MODEL, 1024), lambda i, j: (0, j)),
                  pl.BlockSpec((D_MODEL, LANES), lambda i, j: (0, 0)),
                  pl.BlockSpec((1, LANES), lambda i, j: (0, 0)),
                  pl.BlockSpec((1, LANES), lambda i, j: (0, 0))],
        out_specs=[pl.BlockSpec((tm, 1024), lambda i, j: (i, j)),
                   pl.BlockSpec((tm, LANES), lambda i, j: (i, 0)),
                   pl.BlockSpec((16, tm), lambda i, j: (0, i))],
        out_shape=[jax.ShapeDtypeStruct((m, nj * 1024), BF16),
                   jax.ShapeDtypeStruct((m, LANES), F32),
                   jax.ShapeDtypeStruct((16, m), F32)],
        scratch_shapes=[pltpu.VMEM((tm, D_MODEL), BF16)],
        compiler_params=_cparams(("parallel", "arbitrary")),
        name="in_proj",
    )(x2d, mod3, n1, wmain, wab, alog, dtb)


def _conv_kernel(main_ref, prev_ref, next_ref, w_ref, out_ref, *, tc):
    t = pl.program_id(1)
    nt = pl.num_programs(1)
    part = pl.program_id(2)
    prev = jnp.where(t > 0, prev_ref[...].astype(F32), 0.0)
    nxt = jnp.where(t < nt - 1, next_ref[...].astype(F32), 0.0)
    xe = jnp.concatenate([prev, main_ref[...].astype(F32), nxt], axis=0)
    n = tc + 2 * HALO
    w = w_ref[...]
    acc = None
    for k in range(CONV_K):
        shift = (CONV_K // 2 - k) % n
        xs = xe if shift == 0 else pltpu.roll(xe, shift, 0)
        term = w[k:k + 1, :] * xs[HALO:HALO + tc, :]
        acc = term if acc is None else acc + term
    y = acc * _sigmoid(acc)
    outs = []
    for h in range(GDN_HEADS):
        yh = y[:, h * HEAD_DIM:(h + 1) * HEAD_DIM]
        ss = jnp.sum(yh * yh, axis=-1, keepdims=True)
        outs.append(yh * lax.rsqrt(ss + EPS))
    normed = jnp.concatenate(outs, axis=1)
    scale = jnp.where(part == 0, HEAD_DIM ** -0.5, 1.0)
    out_ref[...] = jnp.where(part < 2, normed * scale, y).astype(BF16)


def _stage_conv(proj, conv_w, nb, t_len, tc):
    m = proj.shape[0]
    nt = t_len // tc
    hb = tc // HALO
    last_halo = m // HALO - 1

    def main_map(b, t, p):
        return (b * nt + t, p)

    def prev_map(b, t, p):
        return (jnp.maximum((b * nt + t) * hb - 1, 0), p)

    def next_map(b, t, p):
        return (jnp.minimum((b * nt + t + 1) * hb, last_halo), p)

    return pl.pallas_call(
        functools.partial(_conv_kernel, tc=tc),
        grid=(nb, nt, 3),
        in_specs=[pl.BlockSpec((tc, GDN_WIDTH), main_map),
                  pl.BlockSpec((HALO, GDN_WIDTH), prev_map),
                  pl.BlockSpec((HALO, GDN_WIDTH), next_map),
                  pl.BlockSpec((CONV_K, GDN_WIDTH), lambda b, t, p: (0, p))],
        out_specs=pl.BlockSpec((tc, GDN_WIDTH), main_map),
        out_shape=jax.ShapeDtypeStruct((m, 3 * GDN_WIDTH), BF16),
        compiler_params=_cparams(("parallel", "parallel", "parallel")),
        name="conv_qkv",
    )(proj, proj, proj, conv_w)


def _tri_inv_all(a_list, ri, ci):
    b16 = (ri // 16) == (ci // 16)
    b32 = (ri // 32) == (ci // 32)
    off32 = jnp.logical_and(b32, jnp.logical_not(b16))
    eye = jnp.where(ri == ci, 1.0, 0.0)
    n16 = [jnp.where(b16, -a, 0.0) for a in a_list]
    t = [eye + n for n in n16]
    p = [n.astype(BF16) for n in n16]
    for _ in range(3):
        p = [_dot(x, x).astype(BF16) for x in p]
        t = [x + _dot(x.astype(BF16), y) for x, y in zip(t, p)]
    for keep in (off32, jnp.logical_not(b32)):
        tb = [x.astype(BF16) for x in t]
        mid = [_dot(x, jnp.where(keep, a, 0.0).astype(BF16)).astype(BF16) for x, a in zip(tb, a_list)]
        t = [x - _dot(y, z) for x, y, z in zip(t, mid, tb)]
    return t


def _gdn_kernel(qf, kf, vf, cgf, rgf, qb, kb, vb, cgb, rgb, of_ref, ob_ref, s_scr):
    hp = pl.program_id(1)

    @pl.when(pl.program_id(2) == 0)
    def _():
        s_scr[...] = jnp.zeros_like(s_scr)

    dirs = ((qf, kf, vf, cgf, rgf, of_ref), (qb, kb, vb, cgb, rgb, ob_ref))
    chains = [(d, hl) for d in range(2) for hl in range(GDN_HP)]
    n_c = SUB // CHUNK
    wide = 3 * LANES * GDN_HP
    rowi = lax.broadcasted_iota(jnp.int32, (LANES, wide), 0)
    coli = lax.broadcasted_iota(jnp.int32, (LANES, wide), 1)
    ri = lax.broadcasted_iota(jnp.int32, (SUB, SUB), 0)
    ci = lax.broadcasted_iota(jnp.int32, (SUB, SUB), 1)
    same = (ri // CHUNK) == (ci // CHUNK)
    incl = (jnp.logical_and(same, ri >= ci), jnp.logical_and(same, ri <= ci))

    bc = []
    for d in range(2):
        lane0 = d * GDN_HEADS + hp * GDN_HP
        target = lane0 + coli // (3 * LANES) + 16 * ((coli // LANES) % 3)
        sel = jnp.where(rowi == target, 1.0, 0.0).astype(BF16)
        c_hi, c_lo = _split(dirs[d][3][...])
        bc.append(_dot(c_hi, sel) + _dot(c_lo, sel))

    q, k, v, gcb, bb, gt, grow = {}, {}, {}, {}, {}, {}, {}
    for d, hl in chains:
        cols = slice(hl * HEAD_DIM, (hl + 1) * HEAD_DIM)
        q[d, hl] = dirs[d][0][:, cols].astype(F32)
        k[d, hl] = dirs[d][1][:, cols].astype(F32)
        v[d, hl] = dirs[d][2][:, cols].astype(F32)
        base = 3 * LANES * hl
        gcb[d, hl] = bc[d][:, base:base + LANES]
        bb[d, hl] = bc[d][:, base + LANES:base + 2 * LANES]
        gt[d, hl] = bc[d][:, base + 2 * LANES:base + 3 * LANES]
        grow[d, hl] = dirs[d][4][pl.ds(d * GDN_HEADS + hp * GDN_HP + hl, 1), :]

    eg = {c: jnp.exp(gcb[c]) for c in chains}
    kbeta = {c: k[c] * bb[c] for c in chains}
    rhs = {c: jnp.concatenate([v[c] * bb[c], kbeta[c] * eg[c]], axis=1).astype(BF16) for c in chains}
    qd = {c: q[c] * eg[c] for c in chains}
    kd = {c: (k[c] * jnp.exp(gt[c] - gcb[c])).astype(BF16) for c in chains}
    dec = {c: jnp.exp(jnp.where(incl[c[0]], jnp.concatenate([gcb[c], gcb[c]], axis=1) - grow[c], NEG))
           for c in chains}
    kq = {c: _dot_nt(jnp.concatenate([kbeta[c], q[c]], axis=0).astype(BF16), k[c].astype(BF16))
          for c in chains}
    a = [jnp.where(ri == ci, 0.0, kq[c][0:SUB] * dec[c]) for c in chains]
    attn = {c: (kq[c][SUB:2 * SUB] * dec[c]).astype(BF16) for c in chains}
    t = _tri_inv_all(a, ri, ci)
    uw = {c: _dot(x.astype(BF16), rhs[c]) for c, x in zip(chains, t)}

    s = {c: s_scr[c[0], c[1]] for c in chains}
    vns = {c: [None] * n_c for c in chains}
    o_s = {c: [None] * n_c for c in chains}
    for step in range(n_c):
        for c in chains:
            ck = step if c[0] == 0 else n_c - 1 - step
            lo, hi = ck * CHUNK, (ck + 1) * CHUNK
            lhs = jnp.concatenate([uw[c][lo:hi, LANES:2 * LANES], qd[c][lo:hi]], axis=0).astype(BF16)
            wq = _dot(lhs, s[c].astype(BF16))
            vn = (uw[c][lo:hi, 0:LANES] - wq[0:CHUNK]).astype(BF16)
            vns[c][ck] = vn
            o_s[c][ck] = wq[CHUNK:2 * CHUNK]
            s[c] = s[c] * jnp.exp(gt[c][lo:lo + 1, :]) + _dot_tn(kd[c][lo:hi], vn)
    for c in chains:
        s_scr[c[0], c[1]] = s[c]
        o = jnp.concatenate(o_s[c], axis=0) + _dot(attn[c], jnp.concatenate(vns[c], axis=0))
        dirs[c[0]][5][:, c[1] * HEAD_DIM:(c[1] + 1) * HEAD_DIM] = o.astype(BF16)


def _stage_gdn(qkvn, colg, rowg, nb, t_len):
    m = qkvn.shape[0]
    ns = t_len // SUB
    width = GDN_HP * HEAD_DIM
    per_part = GDN_HEADS // GDN_HP

    def fwd(b, h, t):
        return b * ns + t

    def bwd(b, h, t):
        return b * ns + (ns - 1 - t)

    def specs(blk):
        return [pl.BlockSpec((SUB, width), lambda b, h, t: (blk(b, h, t), h)),
                pl.BlockSpec((SUB, width), lambda b, h, t: (blk(b, h, t), per_part + h)),
                pl.BlockSpec((SUB, width), lambda b, h, t: (blk(b, h, t), 2 * per_part + h)),
                pl.BlockSpec((SUB, LANES), lambda b, h, t: (blk(b, h, t), 0)),
                pl.BlockSpec((16, SUB), lambda b, h, t: (0, blk(b, h, t)))]

    out_f = pl.BlockSpec((SUB, width), lambda b, h, t: (fwd(b, h, t), h))
    out_b = pl.BlockSpec((SUB, width), lambda b, h, t: (bwd(b, h, t), h))
    return pl.pallas_call(
        _gdn_kernel,
        grid=(nb, per_part, ns),
        in_specs=specs(fwd) + specs(bwd),
        out_specs=[out_f, out_b],
        out_shape=[jax.ShapeDtypeStruct((m, GDN_WIDTH), BF16),
                   jax.ShapeDtypeStruct((m, GDN_WIDTH), BF16)],
        scratch_shapes=[pltpu.VMEM((2, GDN_HP, HEAD_DIM, HEAD_DIM), F32)],
        compiler_params=_cparams(("parallel", "parallel", "arbitrary")),
        name="gdn",
    )(qkvn, qkvn, qkvn, colg, rowg, qkvn, qkvn, qkvn, colg, rowg)


def _out_kernel(of_ref, ob_ref, z_ref, p_ref, pp_ref, pn_ref, band_ref, x_ref, mod_ref,
                gnw_ref, pw_ref, ps_ref, woa_ref, wob_ref, n2_ref, x1_ref, h2t_ref,
                *, tm, tps, t_len):
    i = pl.program_id(0)
    tin = i % tps
    o = of_ref[...].astype(F32) + ob_ref[...].astype(F32)
    z = z_ref[...].astype(F32)
    gnw = gnw_ref[...]
    ya = []
    for h in range(GDN_HEADS):
        oh = o[:, h * HEAD_DIM:(h + 1) * HEAD_DIM]
        ms = jnp.mean(oh * oh, axis=-1, keepdims=True)
        ya.append(oh * lax.rsqrt(ms + EPS) * gnw)
    y_a = jnp.concatenate(ya, axis=1) * (z * _sigmoid(z))

    zero = jnp.zeros((HALO, POOL_WIDTH), BF16)
    prev = jnp.where(tin > 0, pp_ref[...], zero)
    nxt = jnp.where(tin < tps - 1, pn_ref[...], zero)
    p = p_ref[...]
    pe = jnp.concatenate([prev, p, nxt], axis=0)
    tpos = tin * tm + lax.broadcasted_iota(jnp.int32, (tm, POOL_GROUP), 0)
    yb = []
    for gi, win in enumerate(POOL_WINDOWS):
        half = win // 2
        cols = slice(gi * POOL_GROUP, (gi + 1) * POOL_GROUP)
        wsum = _dot(band_ref[gi], pe[:, cols])
        cnt = (jnp.minimum(tpos + half, t_len) - jnp.maximum(tpos - half, 0)).astype(F32)
        pooled = wsum / cnt - p[:, cols].astype(F32)
        yb.append(_dot(pooled.astype(BF16), pw_ref[gi]))
    y_b = jnp.concatenate(yb, axis=1) * ps_ref[...]

    mixed = _dot(y_a.astype(BF16), woa_ref[...]) + _dot(y_b.astype(BF16), wob_ref[...])
    x1 = x_ref[...] + mod_ref[0, 2:3, :] * mixed
    x1_ref[...] = x1
    ms = jnp.mean(x1 * x1, axis=-1, keepdims=True)
    h2 = x1 * lax.rsqrt(ms + EPS) * n2_ref[...]
    h2 = h2 * (1.0 + mod_ref[0, 4:5, :]) + mod_ref[0, 3:4, :]
    h2t_ref[...] = h2.T.astype(BF16)


def _pool_bands(tm):
    t = jnp.arange(tm)[:, None]
    s = jnp.arange(tm + 2 * HALO)[None, :] - HALO
    bands = []
    for win in POOL_WINDOWS:
        half = win // 2
        bands.append(jnp.logical_and(s >= t - half, s <= t + half - 1))
    return jnp.stack(bands).astype(BF16)


def _stage_out(o_f, o_b, proj, x2d, mod3, mod_base, t_len, gnw, pool_w, pool_scale,
               wo_a, wo_b, n2, tm):
    m = x2d.shape[0]
    nt = m // tm
    tps = t_len // tm
    hb = tm // HALO
    last_halo = m // HALO - 1
    pcol = 4 * GDN_WIDTH // POOL_WIDTH
    band = _pool_bands(tm)
    const = lambda i: (0, 0)
    return pl.pallas_call(
        functools.partial(_out_kernel, tm=tm, tps=tps, t_len=t_len),
        grid=(nt,),
        in_specs=[pl.BlockSpec((tm, GDN_WIDTH), lambda i: (i, 0)),
                  pl.BlockSpec((tm, GDN_WIDTH), lambda i: (i, 0)),
                  pl.BlockSpec((tm, GDN_WIDTH), lambda i: (i, 3)),
                  pl.BlockSpec((tm, POOL_WIDTH), lambda i: (i, pcol)),
                  pl.BlockSpec((HALO, POOL_WIDTH), lambda i: (jnp.maximum(i * hb - 1, 0), pcol)),
                  pl.BlockSpec((HALO, POOL_WIDTH), lambda i: (jnp.minimum((i + 1) * hb, last_halo), pcol)),
                  pl.BlockSpec((4, tm, tm + 2 * HALO), lambda i: (0, 0, 0)),
                  pl.BlockSpec((tm, D_MODEL), lambda i: (i, 0)),
                  pl.BlockSpec((1, 6, D_MODEL), lambda i: (mod_base + i // tps, 0, 0)),
                  pl.BlockSpec((1, HEAD_DIM), const),
                  pl.BlockSpec((4, POOL_GROUP, POOL_GROUP), lambda i: (0, 0, 0)),
                  pl.BlockSpec((1, POOL_WIDTH), const),
                  pl.BlockSpec((GDN_WIDTH, D_MODEL), const),
                  pl.BlockSpec((POOL_WIDTH, D_MODEL), const),
                  pl.BlockSpec((1, D_MODEL), const)],
        out_specs=[pl.BlockSpec((tm, D_MODEL), lambda i: (i, 0)),
                   pl.BlockSpec((D_MODEL, tm), lambda i: (0, i))],
        out_shape=[jax.ShapeDtypeStruct((m, D_MODEL), F32),
                   jax.ShapeDtypeStruct((D_MODEL, m), BF16)],
        compiler_params=_cparams(("parallel",)),
        name="mix_out",
    )(o_f, o_b, proj, proj, proj, proj, band, x2d, mod3, gnw, pool_w, pool_scale, wo_a, wo_b, n2)


def _top_rows(cur, n, scr):
    for r in range(n):
        mx = jnp.max(cur, axis=0, keepdims=True)
        scr[r:r + 1, :] = mx
        cur = jnp.where(cur >= mx, NEG, cur)


def _route_kernel(h2t_ref, wqt_ref, khi_ref, klo_ref, lim_ref, r2_ref, w_ref, e2_ref,
                  qt_scr, a_scr, b_scr, c_scr):
    qt_scr[...] = _dot(wqt_ref[...], h2t_ref[...])
    for h in range(PEER_HEADS):
        sc = []
        for p in range(2):
            hp = 2 * h + p
            qh, ql = _split(qt_scr[hp * N_KEYS:(hp + 1) * N_KEYS, :])
            sc.append(_dot(khi_ref[hp], qh) + _dot(khi_ref[hp], ql) + _dot(klo_ref[hp], qh))
        s1, s2 = sc
        _top_rows(s1, PEER_TOPK, a_scr)
        _top_rows(s2, PEER_TOPK, b_scr)
        a = a_scr[...]
        b = b_scr[...]
        cands = [a[0:1, :] + b[0:8, :], a[0:1, :] + b[8:16, :]]
        for p in range(1, 8):
            cands.append(a[p:p + 1, :] + b[0:8, :])
        cands.append(a[8:16, :] + b[0:1, :])
        _top_rows(jnp.concatenate(cands, axis=0), PEER_TOPK + 1, c_scr)
        c = c_scr[0:PEER_TOPK + 1, :]
        zsum = jnp.sum(jnp.exp(c[0:PEER_TOPK, :] - c[0:1, :]), axis=0, keepdims=True)
        tau = 0.5 * (c[PEER_TOPK - 1:PEER_TOPK, :] + c[PEER_TOPK:PEER_TOPK + 1, :])
        lim = jnp.zeros_like(s1)
        r2 = jnp.zeros_like(s2)
        for qi in range(PEER_TOPK):
            bq = b[qi:qi + 1, :]
            lim = lim + jnp.where(s1 + bq >= tau, 1.0, 0.0)
            r2 = r2 + jnp.where(s2 < bq, 1.0, 0.0)
        lim_ref[h] = jnp.where(s1 >= a[PEER_TOPK - 1:PEER_TOPK, :], lim, 0.0)
        r2_ref[h] = r2.astype(BF16)
        w_ref[h] = 0.5 * jnp.exp(s1 - a[0:1, :]) / zsum
        e2_ref[h] = jnp.exp(s2 - b[0:1, :]).astype(BF16)


def _stage_route(h2t, wqt, k_hi, k_lo, tm):
    m = h2t.shape[1]
    nt = m // tm
    rspec = pl.BlockSpec((PEER_HEADS, N_KEYS, tm), lambda i: (0, 0, i))
    row_shape = jax.ShapeDtypeStruct((PEER_HEADS, N_KEYS, m), F32)
    col_shape = jax.ShapeDtypeStruct((PEER_HEADS, N_KEYS, m), BF16)
    return pl.pallas_call(
        _route_kernel,
        grid=(nt,),
        in_specs=[pl.BlockSpec((D_MODEL, tm), lambda i: (0, i)),
                  pl.BlockSpec((D_MODEL, D_MODEL), lambda i: (0, 0)),
                  pl.BlockSpec((2 * PEER_HEADS, N_KEYS, N_KEYS), lambda i: (0, 0, 0)),
                  pl.BlockSpec((2 * PEER_HEADS, N_KEYS, N_KEYS), lambda i: (0, 0, 0))],
        out_specs=[rspec, rspec, rspec, rspec],
        out_shape=[row_shape, col_shape, row_shape, col_shape],
        scratch_shapes=[pltpu.VMEM((D_MODEL, tm), F32),
                        pltpu.VMEM((PEER_TOPK, tm), F32),
                        pltpu.VMEM((PEER_TOPK, tm), F32),
                        pltpu.VMEM((24, tm), F32)],
        compiler_params=_cparams(("parallel",)),
        name="peer_route",
    )(h2t, wqt, k_hi, k_lo)


def _peer_kernel(h2t_ref, u_ref, vt_ref, lim_ref, r2_ref, w_ref, e2_ref, x1_ref, mod_ref,
                 nf_ref, y_ref, acc_scr, hu_scr, act_scr, bl_scr, bw_scr):
    e = pl.program_id(1)
    tm = h2t_ref.shape[1]
    n_sub = PEER_TE // N_KEYS
    pack = 16

    @pl.when(e == 0)
    def _():
        acc_scr[...] = jnp.zeros_like(acc_scr)

    hu_scr[...] = _dot(u_ref[...], h2t_ref[...])
    for ii in range(n_sub):
        row = (e % (8 // n_sub)) * n_sub + ii
        for h in range(PEER_HEADS):
            bl_scr[h] = jnp.broadcast_to(lim_ref[h, pl.ds(row, 1), :], (pack, tm)).astype(BF16)
            bw_scr[h] = jnp.broadcast_to(w_ref[h, pl.ds(row, 1), :], (pack, tm)).astype(BF16)
        for rc in range(N_KEYS // PEER_RC):
            jr = slice(rc * PEER_RC, (rc + 1) * PEER_RC)
            er = slice(ii * N_KEYS + rc * PEER_RC, ii * N_KEYS + (rc + 1) * PEER_RC)
            gate = None
            for h in range(PEER_HEADS):
                r2c = r2_ref[h, jr, :].reshape(PEER_RC // pack, pack, tm)
                e2c = e2_ref[h, jr, :].reshape(PEER_RC // pack, pack, tm)
                term = jnp.where(r2c < bl_scr[h][None], e2c * bw_scr[h][None], jnp.zeros_like(e2c))
                gate = term if gate is None else gate + term
            x = hu_scr[er, :]
            y = (x * (1.0 + lax.erf(x * (2.0 ** -0.5)))).astype(BF16)
            act_scr[er, :] = y * gate.reshape(PEER_RC, tm)
    acc_scr[...] += _dot(vt_ref[...], act_scr[...])

    @pl.when(e == pl.num_programs(1) - 1)
    def _():
        xo = x1_ref[...] + mod_ref[0, 5:6, :] * acc_scr[...].T
        ms = jnp.mean(xo * xo, axis=-1, keepdims=True)
        y_ref[...] = xo * lax.rsqrt(ms + EPS) * nf_ref[...]


def _stage_peer(h2t, u_bf, vt_bf, route, x1, mod3, mod_base, t_len, nf, tm):
    m = x1.shape[0]
    nt = m // tm
    ne = u_bf.shape[0] // PEER_TE
    tps = t_len // tm
    per_blk = 8 * N_KEYS // PEER_TE
    row_spec = pl.BlockSpec((PEER_HEADS, 8, tm), lambda i, e: (0, e // per_blk, i))
    col_spec = pl.BlockSpec((PEER_HEADS, N_KEYS, tm), lambda i, e: (0, 0, i))
    return pl.pallas_call(
        _peer_kernel,
        grid=(nt, ne),
        in_specs=[pl.BlockSpec((D_MODEL, tm), lambda i, e: (0, i)),
                  pl.BlockSpec((PEER_TE, D_MODEL), lambda i, e: (e, 0)),
                  pl.BlockSpec((D_MODEL, PEER_TE), lambda i, e: (0, e)),
                  row_spec, col_spec, row_spec, col_spec,
                  pl.BlockSpec((tm, D_MODEL), lambda i, e: (i, 0)),
                  pl.BlockSpec((1, 6, D_MODEL), lambda i, e: (mod_base + i // tps, 0, 0)),
                  pl.BlockSpec((1, D_MODEL), lambda i, e: (0, 0))],
        out_specs=pl.BlockSpec((tm, D_MODEL), lambda i, e: (i, 0)),
        out_shape=jax.ShapeDtypeStruct((m, D_MODEL), F32),
        scratch_shapes=[pltpu.VMEM((D_MODEL, tm), F32),
                        pltpu.VMEM((PEER_TE, tm), F32),
                        pltpu.VMEM((PEER_TE, tm), BF16),
                        pltpu.VMEM((PEER_HEADS, 16, tm), BF16),
                        pltpu.VMEM((PEER_HEADS, 16, tm), BF16)],
        compiler_params=_cparams(("parallel", "arbitrary")),
        name="peer_dense",
    )(h2t, u_bf, vt_bf, *route, x1, mod3, nf)


def _prepare_weights(norm1_w, w_in, conv_w, a_log, dt_bias, gdn_norm_w, pool_w, pool_scale,
                     w_out, norm2_w, peer_wq, peer_keys, expert_u, expert_v, norm_f_w):
    o0 = 3 * GDN_WIDTH
    o1 = o0 + GDN_WIDTH
    o2 = o1 + 2 * GDN_HEADS
    o3 = o2 + 2 * GDN_HEADS
    w_in = w_in[0]
    wmain = jnp.concatenate([w_in[:, :o1], w_in[:, o3:]], axis=1).astype(BF16)
    wa, wb = w_in[:, o1:o2], w_in[:, o2:o3]
    pad = jnp.zeros((D_MODEL, LANES - 48), F32)
    wab = jnp.concatenate([wa, wb, wa, pad], axis=1).astype(BF16)
    z16 = jnp.zeros((16,), F32)
    z80 = jnp.zeros((LANES - 48,), F32)
    al = a_log[0].reshape(16)
    dtb = dt_bias[0].reshape(16)
    alog = jnp.concatenate([al, z16, al, z80]).reshape(1, LANES)
    dtbias = jnp.concatenate([dtb, z16, dtb, z80]).reshape(1, LANES)
    keys = peer_keys[0].reshape(2 * PEER_HEADS, N_KEYS, N_KEYS)
    k_hi = keys.astype(BF16)
    k_lo = (keys - k_hi.astype(F32)).astype(BF16)
    return dict(
        n1=norm1_w[0].reshape(1, D_MODEL), wmain=wmain, wab=wab, alog=alog, dtb=dtbias,
        conv_w=conv_w[0], gnw=gdn_norm_w[0].reshape(1, HEAD_DIM),
        pool_w=pool_w[0].astype(BF16), pool_scale=pool_scale[0].reshape(1, POOL_WIDTH),
        wo_a=w_out[0][:GDN_WIDTH].astype(BF16), wo_b=w_out[0][GDN_WIDTH:].astype(BF16),
        n2=norm2_w[0].reshape(1, D_MODEL), wqt=peer_wq[0].astype(BF16).T, k_hi=k_hi, k_lo=k_lo,
        u=expert_u[0].astype(BF16), vt=expert_v[0].astype(BF16).T,
        nf=norm_f_w.reshape(1, D_MODEL))


def _trunk_group(x, mod3, mod_base, wts):
    nb, t_len, _ = x.shape
    m = nb * t_len
    x2d = x.reshape(m, D_MODEL)
    tm = TOKEN_TILE
    proj, colg, rowg = _stage_in(x2d, mod3, mod_base, t_len, wts["n1"], wts["wmain"],
                                 wts["wab"], wts["alog"], wts["dtb"], tm)
    qkvn = _stage_conv(proj, wts["conv_w"], nb, t_len, tm)
    o_f, o_b = _stage_gdn(qkvn, colg, rowg, nb, t_len)
    x1, h2t = _stage_out(o_f, o_b, proj, x2d, mod3, mod_base, t_len, wts["gnw"], wts["pool_w"],
                         wts["pool_scale"], wts["wo_a"], wts["wo_b"], wts["n2"], tm)
    route = _stage_route(h2t, wts["wqt"], wts["k_hi"], wts["k_lo"], tm)
    y = _stage_peer(h2t, wts["u"], wts["vt"], route, x1, mod3, mod_base, t_len, wts["nf"], tm)
    return y.reshape(nb, t_len, D_MODEL)


def kernel(x_prompt, x_sample, c_prompt, c_sample, w_ada, b_ada, norm1_w, w_in, conv_w, a_log,
           dt_bias, gdn_norm_w, pool_w, pool_scale, w_out, norm2_w, peer_wq, peer_keys,
           expert_u, expert_v, norm_f_w):
    nbp = c_prompt.shape[0]
    nbs = c_sample.shape[0]
    c8 = jnp.concatenate([c_prompt, c_sample, jnp.zeros((8 - nbp - nbs, D_MODEL), F32)], axis=0)
    mod3 = _ada(c8, w_ada[0], b_ada[0]).reshape(8, 6, D_MODEL)
    wts = _prepare_weights(norm1_w, w_in, conv_w, a_log, dt_bias, gdn_norm_w, pool_w, pool_scale,
                           w_out, norm2_w, peer_wq, peer_keys, expert_u, expert_v, norm_f_w)
    y_prompt = _trunk_group(x_prompt, mod3, 0, wts)
    y_sample = _trunk_group(x_sample, mod3, nbp, wts)
    return (y_prompt, y_sample)
```

```python
import functools

import jax
import jax.numpy as jnp
from jax import lax
from jax.experimental import pallas as pl
from jax.experimental.pallas import tpu as pltpu

F32 = jnp.float32
BF16 = jnp.bfloat16

D_MODEL = 2048
GDN_HEADS = 8
HEAD_DIM = 128
GDN_WIDTH = GDN_HEADS * HEAD_DIM
POOL_WIDTH = 1024
POOL_WINDOWS = (2, 4, 8, 16)
POOL_GROUP = 256
CONV_K = 5
CHUNK = 64
PEER_HEADS = 8
N_KEYS = 128
PEER_TOPK = 16
EPS = 1e-6

LANES = 128
HALO = 16
SUB = 256
GDN_HP = 4
TOKEN_TILE = 512
PEER_TE = 512
PEER_RC = 64
VMEM_LIMIT = 56 * 1024 * 1024
NEG = -1e30


def _cparams(sem):
    return pltpu.CompilerParams(dimension_semantics=sem, vmem_limit_bytes=VMEM_LIMIT)


def _sigmoid(x):
    return 1.0 / (1.0 + jnp.exp(-x))


def _softplus(x):
    return jnp.maximum(x, 0.0) + jnp.log1p(jnp.exp(-jnp.abs(x)))


def _split(x):
    hi = x.astype(BF16)
    lo = (x - hi.astype(F32)).astype(BF16)
    return hi, lo


def _dot(a, b):
    return jnp.dot(a, b, preferred_element_type=F32)


def _dot_nt(a, b):
    return lax.dot_general(a, b, (((1,), (1,)), ((), ())), preferred_element_type=F32)


def _dot_tn(a, b):
    return lax.dot_general(a, b, (((0,), (0,)), ((), ())), preferred_element_type=F32)


def _ada_kernel(c_ref, w_ref, b_ref, o_ref):
    c = c_ref[...]
    sc = c * _sigmoid(c)
    o_ref[...] = jnp.dot(sc, w_ref[...], preferred_element_type=F32,
                         precision=lax.Precision.HIGHEST) + b_ref[...]


def _ada(c8, w_ada, b_ada):
    n = w_ada.shape[1]
    tn = 1024
    return pl.pallas_call(
        _ada_kernel,
        grid=(n // tn,),
        in_specs=[pl.BlockSpec((8, D_MODEL), lambda j: (0, 0)),
                  pl.BlockSpec((D_MODEL, tn), lambda j: (0, j)),
                  pl.BlockSpec((1, tn), lambda j: (0, j))],
        out_specs=pl.BlockSpec((8, tn), lambda j: (0, j)),
        out_shape=jax.ShapeDtypeStruct((8, n), F32),
        compiler_params=_cparams(("arbitrary",)),
        name="ada",
    )(c8, w_ada, b_ada.reshape(1, n))


def _chunk_cumsum(g, axis, reverse):
    n = g.shape[axis]
    pos = lax.broadcasted_iota(jnp.int32, g.shape, axis) % CHUNK
    acc = g
    s = 1
    while s < CHUNK:
        if reverse:
            sh = pltpu.roll(acc, n - s, axis)
            acc = acc + jnp.where(pos < CHUNK - s, sh, 0.0)
        else:
            sh = pltpu.roll(acc, s, axis)
            acc = acc + jnp.where(pos >= s, sh, 0.0)
        s *= 2
    return acc


def _in_kernel(x_ref, mod_ref, n1_ref, wmain_ref, wab_ref, alog_ref, dtb_ref,
               proj_ref, colg_ref, rowg_ref, h_scr):
    j = pl.program_id(1)

    @pl.when(j == 0)
    def _():
        x = x_ref[...]
        ms = jnp.mean(x * x, axis=-1, keepdims=True)
        y = x * lax.rsqrt(ms + EPS) * n1_ref[...]
        h = y * (1.0 + mod_ref[0, 1:2, :]) + mod_ref[0, 0:1, :]
        hb = h.astype(BF16)
        h_scr[...] = hb
        ab = _dot(hb, wab_ref[...])
        g = -jnp.exp(alog_ref[...]) * _softplus(ab + dtb_ref[...])
        beta = _sigmoid(ab)
        cum_f = _chunk_cumsum(g, 0, False)
        cum_b = _chunk_cumsum(g, 0, True)
        lane = lax.broadcasted_iota(jnp.int32, g.shape, 1)
        gc = jnp.where(lane % 16 < GDN_HEADS, cum_f, cum_b)
        gtot = cum_f + cum_b - g
        colg = jnp.where(lane < 16, gc, jnp.where(lane < 32, beta, jnp.where(lane < 48, gtot, 0.0)))
        colg_ref[...] = colg
        rowg_ref[...] = colg.T[0:16, :]

    proj_ref[...] = _dot(h_scr[...], wmain_ref[...]).astype(BF16)


def _stage_in(x2d, mod3, mod_base, t_len, n1, wmain, wab, alog, dtb, tm):
    m = x2d.shape[0]
    nt = m // tm
    nj = wmain.shape[1] // 1024
    tps = t_len // tm
    return pl.pallas_call(
        _in_kernel,
        grid=(nt, nj),
        in_specs=[pl.BlockSpec((tm, D_MODEL), lambda i, j: (i, 0)),
                  pl.BlockSpec((1, 6, D_MODEL), lambda i, j: (mod_base + i // tps, 0, 0)),
                  pl.BlockSpec((1, D_MODEL), lambda i, j: (0, 0)),
                  pl.BlockSpec((D_MODEL, 1024), lambda i, j: (0, j)),
                  pl.BlockSpec((D_MODEL, LANES), lambda i, j: (0, 0)),
                  pl.BlockSpec((1, LANES), lambda i, j: (0, 0)),
                  pl.BlockSpec((1, LANES), lambda i, j: (0, 0))],
        out_specs=[pl.BlockSpec((tm, 1024), lambda i, j: (i, j)),
                   pl.BlockSpec((tm, LANES), lambda i, j: (i, 0)),
                   pl.BlockSpec((16, tm), lambda i, j: (0, i))],
        out_shape=[jax.ShapeDtypeStruct((m, nj * 1024), BF16),
                   jax.ShapeDtypeStruct((m, LANES), F32),
                   jax.ShapeDtypeStruct((16, m), F32)],
        scratch_shapes=[pltpu.VMEM((tm, D_MODEL), BF16)],
        compiler_params=_cparams(("parallel", "arbitrary")),
        name="in_proj",
    )(x2d, mod3, n1, wmain, wab, alog, dtb)


def _conv_kernel(main_ref, prev_ref, next_ref, w_ref, out_ref, *, tc):
    t = pl.program_id(1)
    nt = pl.num_programs(1)
    part = pl.program_id(2)
    prev = jnp.where(t > 0, prev_ref[...].astype(F32), 0.0)
    nxt = jnp.where(t < nt - 1, next_ref[...].astype(F32), 0.0)
    xe = jnp.concatenate([prev, main_ref[...].astype(F32), nxt], axis=0)
    n = tc + 2 * HALO
    w = w_ref[...]
    acc = None
    for k in range(CONV_K):
        shift = (CONV_K // 2 - k) % n
        xs = xe if shift == 0 else pltpu.roll(xe, shift, 0)
        term = w[k:k + 1, :] * xs[HALO:HALO + tc, :]
        acc = term if acc is None else acc + term
    y = acc * _sigmoid(acc)
    outs = []
    for h in range(GDN_HEADS):
        yh = y[:, h * HEAD_DIM:(h + 1) * HEAD_DIM]
        ss = jnp.sum(yh * yh, axis=-1, keepdims=True)
        outs.append(yh * lax.rsqrt(ss + EPS))
    normed = jnp.concatenate(outs, axis=1)
    scale = jnp.where(part == 0, HEAD_DIM ** -0.5, 1.0)
    out_ref[...] = jnp.where(part < 2, normed * scale, y).astype(BF16)


def _stage_conv(proj, conv_w, nb, t_len, tc):
    m = proj.shape[0]
    nt = t_len // tc
    hb = tc // HALO
    last_halo = m // HALO - 1

    def main_map(b, t, p):
        return (b * nt + t, p)

    def prev_map(b, t, p):
        return (jnp.maximum((b * nt + t) * hb - 1, 0), p)

    def next_map(b, t, p):
        return (jnp.minimum((b * nt + t + 1) * hb, last_halo), p)

    return pl.pallas_call(
        functools.partial(_conv_kernel, tc=tc),
        grid=(nb, nt, 3),
        in_specs=[pl.BlockSpec((tc, GDN_WIDTH), main_map),
                  pl.BlockSpec((HALO, GDN_WIDTH), prev_map),
                  pl.BlockSpec((HALO, GDN_WIDTH), next_map),
                  pl.BlockSpec((CONV_K, GDN_WIDTH), lambda b, t, p: (0, p))],
        out_specs=pl.BlockSpec((tc, GDN_WIDTH), main_map),
        out_shape=jax.ShapeDtypeStruct((m, 3 * GDN_WIDTH), BF16),
        compiler_params=_cparams(("parallel", "parallel", "parallel")),
        name="conv_qkv",
    )(proj, proj, proj, conv_w)


def _tri_inv_all(a_list, ri, ci):
    b16 = (ri // 16) == (ci // 16)
    b32 = (ri // 32) == (ci // 32)
    off32 = jnp.logical_and(b32, jnp.logical_not(b16))
    eye = jnp.where(ri == ci, 1.0, 0.0)
    n16 = [jnp.where(b16, -a, 0.0) for a in a_list]
    t = [eye + n for n in n16]
    p = [n.astype(BF16) for n in n16]
    for _ in range(3):
        p = [_dot(x, x).astype(BF16) for x in p]
        t = [x + _dot(x.astype(BF16), y) for x, y in zip(t, p)]
    for keep in (off32, jnp.logical_not(b32)):
        tb = [x.astype(BF16) for x in t]
        mid = [_dot(x, jnp.where(keep, a, 0.0).astype(BF16)).astype(BF16) for x, a in zip(tb, a_list)]
        t = [x - _dot(y, z) for x, y, z in zip(t, mid, tb)]
    return t


def _gdn_kernel(qf, kf, vf, cgf, rgf, qb, kb, vb, cgb, rgb, of_ref, ob_ref, s_scr):
    hp = pl.program_id(1)

    @pl.when(pl.program_id(2) == 0)
    def _():
        s_scr[...] = jnp.zeros_like(s_scr)

    dirs = ((qf, kf, vf, cgf, rgf, of_ref), (qb, kb, vb, cgb, rgb, ob_ref))
    chains = [(d, hl) for d in range(2) for hl in range(GDN_HP)]
    n_c = SUB // CHUNK
    wide = 3 * LANES * GDN_HP
    rowi = lax.broadcasted_iota(jnp.int32, (LANES, wide), 0)
    coli = lax.broadcasted_iota(jnp.int32, (LANES, wide), 1)
    ri = lax.broadcasted_iota(jnp.int32, (SUB, SUB), 0)
    ci = lax.broadcasted_iota(jnp.int32, (SUB, SUB), 1)
    same = (ri // CHUNK) == (ci // CHUNK)
    incl = (jnp.logical_and(same, ri >= ci), jnp.logical_and(same, ri <= ci))

    bc = []
    for d in range(2):
        lane0 = d * GDN_HEADS + hp * GDN_HP
        target = lane0 + coli // (3 * LANES) + 16 * ((coli // LANES) % 3)
        sel = jnp.where(rowi == target, 1.0, 0.0).astype(BF16)
        c_hi, c_lo = _split(dirs[d][3][...])
        bc.append(_dot(c_hi, sel) + _dot(c_lo, sel))

    q, k, v, gcb, bb, gt, grow = {}, {}, {}, {}, {}, {}, {}
    for d, hl in chains:
        cols = slice(hl * HEAD_DIM, (hl + 1) * HEAD_DIM)
        q[d, hl] = dirs[d][0][:, cols].astype(F32)
        k[d, hl] = dirs[d][1][:, cols].astype(F32)
        v[d, hl] = dirs[d][2][:, cols].astype(F32)
        base = 3 * LANES * hl
        gcb[d, hl] = bc[d][:, base:base + LANES]
        bb[d, hl] = bc[d][:, base + LANES:base + 2 * LANES]
        gt[d, hl] = bc[d][:, base + 2 * LANES:base + 3 * LANES]
        grow[d, hl] = dirs[d][4][pl.ds(d * GDN_HEADS + hp * GDN_HP + hl, 1), :]

    eg = {c: jnp.exp(gcb[c]) for c in chains}
    kbeta = {c: k[c] * bb[c] for c in chains}
    rhs = {c: jnp.concatenate([v[c] * bb[c], kbeta[c] * eg[c]], axis=1).astype(BF16) for c in chains}
    qd = {c: q[c] * eg[c] for c in chains}
    kd = {c: (k[c] * jnp.exp(gt[c] - gcb[c])).astype(BF16) for c in chains}
    dec = {c: jnp.exp(jnp.where(incl[c[0]], jnp.concatenate([gcb[c], gcb[c]], axis=1) - grow[c], NEG))
           for c in chains}
    kq = {c: _dot_nt(jnp.concatenate([kbeta[c], q[c]], axis=0).astype(BF16), k[c].astype(BF16))
          for c in chains}
    a = [jnp.where(ri == ci, 0.0, kq[c][0:SUB] * dec[c]) for c in chains]
    attn = {c: (kq[c][SUB:2 * SUB] * dec[c]).astype(BF16) for c in chains}
    t = _tri_inv_all(a, ri, ci)
    uw = {c: _dot(x.astype(BF16), rhs[c]) for c, x in zip(chains, t)}

    s = {c: s_scr[c[0], c[1]] for c in chains}
    vns = {c: [None] * n_c for c in chains}
    o_s = {c: [None] * n_c for c in chains}
    for step in range(n_c):
        for c in chains:
            ck = step if c[0] == 0 else n_c - 1 - step
            lo, hi = ck * CHUNK, (ck + 1) * CHUNK
            lhs = jnp.concatenate([uw[c][lo:hi, LANES:2 * LANES], qd[c][lo:hi]], axis=0).astype(BF16)
            wq = _dot(lhs, s[c].astype(BF16))
            vn = (uw[c][lo:hi, 0:LANES] - wq[0:CHUNK]).astype(BF16)
            vns[c][ck] = vn
            o_s[c][ck] = wq[CHUNK:2 * CHUNK]
            s[c] = s[c] * jnp.exp(gt[c][lo:lo + 1, :]) + _dot_tn(kd[c][lo:hi], vn)
    for c in chains:
        s_scr[c[0], c[1]] = s[c]
        o = jnp.concatenate(o_s[c], axis=0) + _dot(attn[c], jnp.concatenate(vns[c], axis=0))
        dirs[c[0]][5][:, c[1] * HEAD_DIM:(c[1] + 1) * HEAD_DIM] = o.astype(BF16)


def _stage_gdn(qkvn, colg, rowg, nb, t_len):
    m = qkvn.shape[0]
    ns = t_len // SUB
    width = GDN_HP * HEAD_DIM
    per_part = GDN_HEADS // GDN_HP

    def fwd(b, h, t):
        return b * ns + t

    def bwd(b, h, t):
        return b * ns + (ns - 1 - t)

    def specs(blk):
        return [pl.BlockSpec((SUB, width), lambda b, h, t: (blk(b, h, t), h)),
                pl.BlockSpec((SUB, width), lambda b, h, t: (blk(b, h, t), per_part + h)),
                pl.BlockSpec((SUB, width), lambda b, h, t: (blk(b, h, t), 2 * per_part + h)),
                pl.BlockSpec((SUB, LANES), lambda b, h, t: (blk(b, h, t), 0)),
                pl.BlockSpec((16, SUB), lambda b, h, t: (0, blk(b, h, t)))]

    out_f = pl.BlockSpec((SUB, width), lambda b, h, t: (fwd(b, h, t), h))
    out_b = pl.BlockSpec((SUB, width), lambda b, h, t: (bwd(b, h, t), h))
    return pl.pallas_call(
        _gdn_kernel,
        grid=(nb, per_part, ns),
        in_specs=specs(fwd) + specs(bwd),
        out_specs=[out_f, out_b],
        out_shape=[jax.ShapeDtypeStruct((m, GDN_WIDTH), BF16),
                   jax.ShapeDtypeStruct((m, GDN_WIDTH), BF16)],
        scratch_shapes=[pltpu.VMEM((2, GDN_HP, HEAD_DIM, HEAD_DIM), F32)],
        compiler_params=_cparams(("parallel", "parallel", "arbitrary")),
        name="gdn",
    )(qkvn, qkvn, qkvn, colg, rowg, qkvn, qkvn, qkvn, colg, rowg)


def _out_kernel(of_ref, ob_ref, z_ref, p_ref, pp_ref, pn_ref, band_ref, x_ref, mod_ref,
                gnw_ref, pw_ref, ps_ref, woa_ref, wob_ref, n2_ref, x1_ref, h2t_ref,
                *, tm, tps, t_len):
    i = pl.program_id(0)
    tin = i % tps
    o = of_ref[...].astype(F32) + ob_ref[...].astype(F32)
    z = z_ref[...].astype(F32)
    gnw = gnw_ref[...]
    ya = []
    for h in range(GDN_HEADS):
        oh = o[:, h * HEAD_DIM:(h + 1) * HEAD_DIM]
        ms = jnp.mean(oh * oh, axis=-1, keepdims=True)
        ya.append(oh * lax.rsqrt(ms + EPS) * gnw)
    y_a = jnp.concatenate(ya, axis=1) * (z * _sigmoid(z))

    zero = jnp.zeros((HALO, POOL_WIDTH), BF16)
    prev = jnp.where(tin > 0, pp_ref[...], zero)
    nxt = jnp.where(tin < tps - 1, pn_ref[...], zero)
    p = p_ref[...]
    pe = jnp.concatenate([prev, p, nxt], axis=0)
    tpos = tin * tm + lax.broadcasted_iota(jnp.int32, (tm, POOL_GROUP), 0)
    yb = []
    for gi, win in enumerate(POOL_WINDOWS):
        half = win // 2
        cols = slice(gi * POOL_GROUP, (gi + 1) * POOL_GROUP)
        wsum = _dot(band_ref[gi], pe[:, cols])
        cnt = (jnp.minimum(tpos + half, t_len) - jnp.maximum(tpos - half, 0)).astype(F32)
        pooled = wsum / cnt - p[:, cols].astype(F32)
        yb.append(_dot(pooled.astype(BF16), pw_ref[gi]))
    y_b = jnp.concatenate(yb, axis=1) * ps_ref[...]

    mixed = _dot(y_a.astype(BF16), woa_ref[...]) + _dot(y_b.astype(BF16), wob_ref[...])
    x1 = x_ref[...] + mod_ref[0, 2:3, :] * mixed
    x1_ref[...] = x1
    ms = jnp.mean(x1 * x1, axis=-1, keepdims=True)
    h2 = x1 * lax.rsqrt(ms + EPS) * n2_ref[...]
    h2 = h2 * (1.0 + mod_ref[0, 4:5, :]) + mod_ref[0, 3:4, :]
    h2t_ref[...] = h2.T.astype(BF16)


def _pool_bands(tm):
    t = jnp.arange(tm)[:, None]
    s = jnp.arange(tm + 2 * HALO)[None, :] - HALO
    bands = []
    for win in POOL_WINDOWS:
        half = win // 2
        bands.append(jnp.logical_and(s >= t - half, s <= t + half - 1))
    return jnp.stack(bands).astype(BF16)


def _stage_out(o_f, o_b, proj, x2d, mod3, mod_base, t_len, gnw, pool_w, pool_scale,
               wo_a, wo_b, n2, tm):
    m = x2d.shape[0]
    nt = m // tm
    tps = t_len // tm
    hb = tm // HALO
    last_halo = m // HALO - 1
    pcol = 4 * GDN_WIDTH // POOL_WIDTH
    band = _pool_bands(tm)
    const = lambda i: (0, 0)
    return pl.pallas_call(
        functools.partial(_out_kernel, tm=tm, tps=tps, t_len=t_len),
        grid=(nt,),
        in_specs=[pl.BlockSpec((tm, GDN_WIDTH), lambda i: (i, 0)),
                  pl.BlockSpec((tm, GDN_WIDTH), lambda i: (i, 0)),
                  pl.BlockSpec((tm, GDN_WIDTH), lambda i: (i, 3)),
                  pl.BlockSpec((tm, POOL_WIDTH), lambda i: (i, pcol)),
                  pl.BlockSpec((HALO, POOL_WIDTH), lambda i: (jnp.maximum(i * hb - 1, 0), pcol)),
                  pl.BlockSpec((HALO, POOL_WIDTH), lambda i: (jnp.minimum((i + 1) * hb, last_halo), pcol)),
                  pl.BlockSpec((4, tm, tm + 2 * HALO), lambda i: (0, 0, 0)),
                  pl.BlockSpec((tm, D_MODEL), lambda i: (i, 0)),
                  pl.BlockSpec((1, 6, D_MODEL), lambda i: (mod_base + i // tps, 0, 0)),
                  pl.BlockSpec((1, HEAD_DIM), const),
                  pl.BlockSpec((4, POOL_GROUP, POOL_GROUP), lambda i: (0, 0, 0)),
                  pl.BlockSpec((1, POOL_WIDTH), const),
                  pl.BlockSpec((GDN_WIDTH, D_MODEL), const),
                  pl.BlockSpec((POOL_WIDTH, D_MODEL), const),
                  pl.BlockSpec((1, D_MODEL), const)],
        out_specs=[pl.BlockSpec((tm, D_MODEL), lambda i: (i, 0)),
                   pl.BlockSpec((D_MODEL, tm), lambda i: (0, i))],
        out_shape=[jax.ShapeDtypeStruct((m, D_MODEL), F32),
                   jax.ShapeDtypeStruct((D_MODEL, m), BF16)],
        compiler_params=_cparams(("parallel",)),
        name="mix_out",
    )(o_f, o_b, proj, proj, proj, proj, band, x2d, mod3, gnw, pool_w, pool_scale, wo_a, wo_b, n2)


def _top_rows(cur, n, scr, want_rank=False):
    rows, tm = cur.shape
    cur = cur.reshape(rows // 8, 8, tm)
    rank = jnp.full(cur.shape, float(n), F32) if want_rank else None
    for r in range(n):
        m8 = jnp.max(cur, axis=0)
        for s in (1, 2, 4):
            m8 = jnp.maximum(m8, pltpu.roll(m8, s, 0))
        scr[r:r + 1, :] = m8[0:1, :]
        hit = cur >= m8[None]
        if want_rank:
            rank = jnp.where(hit, float(r), rank)
        cur = jnp.where(hit, NEG, cur)
    return rank.reshape(rows, tm) if want_rank else None


def _route_kernel(h2t_ref, wqt_ref, khi_ref, klo_ref, lim_ref, r2_ref, w_ref, e2_ref,
                  qt_scr, a_scr, b_scr, c_scr):
    qt_scr[...] = _dot(wqt_ref[...], h2t_ref[...])
    for h in range(PEER_HEADS):
        sc = []
        for p in range(2):
            hp = 2 * h + p
            qh, ql = _split(qt_scr[hp * N_KEYS:(hp + 1) * N_KEYS, :])
            sc.append(_dot(khi_ref[hp], qh) + _dot(khi_ref[hp], ql) + _dot(klo_ref[hp], qh))
        s1, s2 = sc
        rank1 = _top_rows(s1, PEER_TOPK, a_scr, want_rank=True)
        rank2 = _top_rows(s2, PEER_TOPK, b_scr, want_rank=True)
        a = a_scr[...]
        b = b_scr[...]
        cands = [a[0:1, :] + b[0:8, :], a[0:1, :] + b[8:16, :]]
        for p in range(1, 8):
            cands.append(a[p:p + 1, :] + b[0:8, :])
        cands.append(a[8:16, :] + b[0:1, :])
        _top_rows(jnp.concatenate(cands, axis=0), PEER_TOPK + 1, c_scr)
        c = c_scr[0:PEER_TOPK + 1, :]
        zsum = jnp.sum(jnp.exp(c[0:PEER_TOPK, :] - c[0:1, :]), axis=0, keepdims=True)
        tau = 0.5 * (c[PEER_TOPK - 1:PEER_TOPK, :] + c[PEER_TOPK:PEER_TOPK + 1, :])
        count = jnp.zeros_like(a)
        for qi in range(PEER_TOPK):
            count = count + jnp.where(a + b[qi:qi + 1, :] >= tau, 1.0, 0.0)
        lim = jnp.zeros_like(s1)
        for p in range(PEER_TOPK):
            lim = jnp.where(rank1 == float(p), count[p:p + 1, :], lim)
        lim_ref[h] = lim
        r2_ref[h] = rank2.astype(BF16)
        w_ref[h] = 0.5 * jnp.exp(s1 - a[0:1, :]) / zsum
        e2_ref[h] = jnp.exp(s2 - b[0:1, :]).astype(BF16)


def _stage_route(h2t, wqt, k_hi, k_lo, tm):
    m = h2t.shape[1]
    nt = m // tm
    rspec = pl.BlockSpec((PEER_HEADS, N_KEYS, tm), lambda i: (0, 0, i))
    row_shape = jax.ShapeDtypeStruct((PEER_HEADS, N_KEYS, m), F32)
    col_shape = jax.ShapeDtypeStruct((PEER_HEADS, N_KEYS, m), BF16)
    return pl.pallas_call(
        _route_kernel,
        grid=(nt,),
        in_specs=[pl.BlockSpec((D_MODEL, tm), lambda i: (0, i)),
                  pl.BlockSpec((D_MODEL, D_MODEL), lambda i: (0, 0)),
                  pl.BlockSpec((2 * PEER_HEADS, N_KEYS, N_KEYS), lambda i: (0, 0, 0)),
                  pl.BlockSpec((2 * PEER_HEADS, N_KEYS, N_KEYS), lambda i: (0, 0, 0))],
        out_specs=[rspec, rspec, rspec, rspec],
        out_shape=[row_shape, col_shape, row_shape, col_shape],
        scratch_shapes=[pltpu.VMEM((D_MODEL, tm), F32),
                        pltpu.VMEM((PEER_TOPK, tm), F32),
                        pltpu.VMEM((PEER_TOPK, tm), F32),
                        pltpu.VMEM((24, tm), F32)],
        compiler_params=_cparams(("parallel",)),
        name="peer_route",
    )(h2t, wqt, k_hi, k_lo)


def _peer_kernel(h2t_ref, u_ref, vt_ref, lim_ref, r2_ref, w_ref, e2_ref, x1_ref, mod_ref,
                 nf_ref, y_ref, acc_scr, hu_scr, act_scr, bl_scr, bw_scr):
    e = pl.program_id(1)
    tm = h2t_ref.shape[1]
    hw = tm // 2
    n_sub = PEER_TE // N_KEYS
    pack = 16
    k_piece = D_MODEL // n_sub
    o_piece = D_MODEL // n_sub
    halves = (slice(0, hw), slice(hw, tm))

    @pl.when(e == 0)
    def _():
        acc_scr[...] = jnp.zeros_like(acc_scr)

    def gate_act(ii, cols):
        row = (e % (8 // n_sub)) * n_sub + ii
        for h in range(PEER_HEADS):
            bl_scr[h, :, cols] = jnp.broadcast_to(lim_ref[h, pl.ds(row, 1), cols], (pack, hw)).astype(BF16)
            bw_scr[h, :, cols] = jnp.broadcast_to(w_ref[h, pl.ds(row, 1), cols], (pack, hw)).astype(BF16)
        for rc in range(N_KEYS // PEER_RC):
            jr = slice(rc * PEER_RC, (rc + 1) * PEER_RC)
            er = slice(ii * N_KEYS + rc * PEER_RC, ii * N_KEYS + (rc + 1) * PEER_RC)
            gate = None
            for h in range(PEER_HEADS):
                r2c = r2_ref[h, jr, cols].reshape(PEER_RC // pack, pack, hw)
                e2c = e2_ref[h, jr, cols].reshape(PEER_RC // pack, pack, hw)
                term = jnp.where(r2c < bl_scr[h, :, cols][None], e2c * bw_scr[h, :, cols][None],
                                 jnp.zeros_like(e2c))
                gate = term if gate is None else gate + term
            x = hu_scr[er, cols]
            y = (x * (1.0 + lax.erf(x * (2.0 ** -0.5)))).astype(BF16)
            act_scr[er, cols] = y * gate.reshape(PEER_RC, hw)

    hu_scr[:, halves[0]] = _dot(u_ref[...], h2t_ref[:, halves[0]])
    for ii in range(n_sub):
        gate_act(ii, halves[0])
        kk = slice(ii * k_piece, (ii + 1) * k_piece)
        part = _dot(u_ref[:, kk], h2t_ref[kk, halves[1]])
        if ii == 0:
            hu_scr[:, halves[1]] = part
        else:
            hu_scr[:, halves[1]] += part
    for ii in range(n_sub):
        gate_act(ii, halves[1])
        rr = slice(ii * o_piece, (ii + 1) * o_piece)
        acc_scr[rr, halves[0]] += _dot(vt_ref[rr, :], act_scr[:, halves[0]])
    acc_scr[:, halves[1]] += _dot(vt_ref[...], act_scr[:, halves[1]])

    @pl.when(e == pl.num_programs(1) - 1)
    def _():
        xo = x1_ref[...] + mod_ref[0, 5:6, :] * acc_scr[...].T
        ms = jnp.mean(xo * xo, axis=-1, keepdims=True)
        y_ref[...] = xo * lax.rsqrt(ms + EPS) * nf_ref[...]


def _stage_peer(h2t, u_bf, vt_bf, route, x1, mod3, mod_base, t_len, nf, tm):
    m = x1.shape[0]
    nt = m // tm
    ne = u_bf.shape[0] // PEER_TE
    tps = t_len // tm
    per_blk = 8 * N_KEYS // PEER_TE
    row_spec = pl.BlockSpec((PEER_HEADS, 8, tm), lambda i, e: (0, e // per_blk, i))
    col_spec = pl.BlockSpec((PEER_HEADS, N_KEYS, tm), lambda i, e: (0, 0, i))
    return pl.pallas_call(
        _peer_kernel,
        grid=(nt, ne),
        in_specs=[pl.BlockSpec((D_MODEL, tm), lambda i, e: (0, i)),
                  pl.BlockSpec((PEER_TE, D_MODEL), lambda i, e: (e, 0)),
                  pl.BlockSpec((D_MODEL, PEER_TE), lambda i, e: (0, e)),
                  row_spec, col_spec, row_spec, col_spec,
                  pl.BlockSpec((tm, D_MODEL), lambda i, e: (i, 0)),
                  pl.BlockSpec((1, 6, D_MODEL), lambda i, e: (mod_base + i // tps, 0, 0)),
                  pl.BlockSpec((1, D_MODEL), lambda i, e: (0, 0))],
        out_specs=pl.BlockSpec((tm, D_MODEL), lambda i, e: (i, 0)),
        out_shape=jax.ShapeDtypeStruct((m, D_MODEL), F32),
        scratch_shapes=[pltpu.VMEM((D_MODEL, tm), F32),
                        pltpu.VMEM((PEER_TE, tm), F32),
                        pltpu.VMEM((PEER_TE, tm), BF16),
                        pltpu.VMEM((PEER_HEADS, 16, tm), BF16),
                        pltpu.VMEM((PEER_HEADS, 16, tm), BF16)],
        compiler_params=_cparams(("parallel", "arbitrary")),
        name="peer_dense",
    )(h2t, u_bf, vt_bf, *route, x1, mod3, nf)


def _prepare_weights(norm1_w, w_in, conv_w, a_log, dt_bias, gdn_norm_w, pool_w, pool_scale,
                     w_out, norm2_w, peer_wq, peer_keys, expert_u, expert_v, norm_f_w):
    o0 = 3 * GDN_WIDTH
    o1 = o0 + GDN_WIDTH
    o2 = o1 + 2 * GDN_HEADS
    o3 = o2 + 2 * GDN_HEADS
    w_in = w_in[0]
    wmain = jnp.concatenate([w_in[:, :o1], w_in[:, o3:]], axis=1).astype(BF16)
    wa, wb = w_in[:, o1:o2], w_in[:, o2:o3]
    pad = jnp.zeros((D_MODEL, LANES - 48), F32)
    wab = jnp.concatenate([wa, wb, wa, pad], axis=1).astype(BF16)
    z16 = jnp.zeros((16,), F32)
    z80 = jnp.zeros((LANES - 48,), F32)
    al = a_log[0].reshape(16)
    dtb = dt_bias[0].reshape(16)
    alog = jnp.concatenate([al, z16, al, z80]).reshape(1, LANES)
    dtbias = jnp.concatenate([dtb, z16, dtb, z80]).reshape(1, LANES)
    keys = peer_keys[0].reshape(2 * PEER_HEADS, N_KEYS, N_KEYS)
    k_hi = keys.astype(BF16)
    k_lo = (keys - k_hi.astype(F32)).astype(BF16)
    return dict(
        n1=norm1_w[0].reshape(1, D_MODEL), wmain=wmain, wab=wab, alog=alog, dtb=dtbias,
        conv_w=conv_w[0], gnw=gdn_norm_w[0].reshape(1, HEAD_DIM),
        pool_w=pool_w[0].astype(BF16), pool_scale=pool_scale[0].reshape(1, POOL_WIDTH),
        wo_a=w_out[0][:GDN_WIDTH].astype(BF16), wo_b=w_out[0][GDN_WIDTH:].astype(BF16),
        n2=norm2_w[0].reshape(1, D_MODEL), wqt=peer_wq[0].astype(BF16).T, k_hi=k_hi, k_lo=k_lo,
        u=expert_u[0].astype(BF16), vt=expert_v[0].astype(BF16).T,
        nf=norm_f_w.reshape(1, D_MODEL))


def _trunk_group(x, mod3, mod_base, wts):
    nb, t_len, _ = x.shape
    m = nb * t_len
    x2d = x.reshape(m, D_MODEL)
    tm = TOKEN_TILE
    proj, colg, rowg = _stage_in(x2d, mod3, mod_base, t_len, wts["n1"], wts["wmain"],
                                 wts["wab"], wts["alog"], wts["dtb"], tm)
    qkvn = _stage_conv(proj, wts["conv_w"], nb, t_len, tm)
    o_f, o_b = _stage_gdn(qkvn, colg, rowg, nb, t_len)
    x1, h2t = _stage_out(o_f, o_b, proj, x2d, mod3, mod_base, t_len, wts["gnw"], wts["pool_w"],
                         wts["pool_scale"], wts["wo_a"], wts["wo_b"], wts["n2"], tm)
    route = _stage_route(h2t, wts["wqt"], wts["k_hi"], wts["k_lo"], tm)
    y = _stage_peer(h2t, wts["u"], wts["vt"], route, x1, mod3, mod_base, t_len, wts["nf"], tm)
    return y.reshape(nb, t_len, D_MODEL)


def kernel(x_prompt, x_sample, c_prompt, c_sample, w_ada, b_ada, norm1_w, w_in, conv_w, a_log,
           dt_bias, gdn_norm_w, pool_w, pool_scale, w_out, norm2_w, peer_wq, peer_keys,
           expert_u, expert_v, norm_f_w):
    nbp = c_prompt.shape[0]
    nbs = c_sample.shape[0]
    c8 = jnp.concatenate([c_prompt, c_sample, jnp.zeros((8 - nbp - nbs, D_MODEL), F32)], axis=0)
    mod3 = _ada(c8, w_ada[0], b_ada[0]).reshape(8, 6, D_MODEL)
    wts = _prepare_weights(norm1_w, w_in, conv_w, a_log, dt_bias, gdn_norm_w, pool_w, pool_scale,
                           w_out, norm2_w, peer_wq, peer_keys, expert_u, expert_v, norm_f_w)
    y_prompt = _trunk_group(x_prompt, mod3, 0, wts)
    y_sample = _trunk_group(x_sample, mod3, nbp, wts)
    return (y_prompt, y_sample)
```

```python
import functools

import jax
import jax.numpy as jnp
from jax import lax
from jax.experimental import pallas as pl
from jax.experimental.pallas import tpu as pltpu

F32 = jnp.float32
BF16 = jnp.bfloat16

D_MODEL = 2048
GDN_HEADS = 8
HEAD_DIM = 128
GDN_WIDTH = GDN_HEADS * HEAD_DIM
POOL_WIDTH = 1024
POOL_WINDOWS = (2, 4, 8, 16)
POOL_GROUP = 256
CONV_K = 5
CHUNK = 64
PEER_HEADS = 8
N_KEYS = 128
PEER_TOPK = 16
EPS = 1e-6

LANES = 128
HALO = 16
SUB = 256
GDN_HP = 4
TOKEN_TILE = 512
PEER_TE = 1024
PEER_RC = 32
VMEM_LIMIT = 56 * 1024 * 1024
NEG = -1e30


def _cparams(sem):
    return pltpu.CompilerParams(dimension_semantics=sem, vmem_limit_bytes=VMEM_LIMIT)


def _sigmoid(x):
    return 1.0 / (1.0 + jnp.exp(-x))


def _softplus(x):
    return jnp.maximum(x, 0.0) + jnp.log1p(jnp.exp(-jnp.abs(x)))


def _split(x):
    hi = x.astype(BF16)
    lo = (x - hi.astype(F32)).astype(BF16)
    return hi, lo


def _dot(a, b):
    return jnp.dot(a, b, preferred_element_type=F32)


def _dot_nt(a, b):
    return lax.dot_general(a, b, (((1,), (1,)), ((), ())), preferred_element_type=F32)


def _dot_tn(a, b):
    return lax.dot_general(a, b, (((0,), (0,)), ((), ())), preferred_element_type=F32)


def _ada_kernel(c_ref, w_ref, b_ref, o_ref):
    c = c_ref[...]
    sc = c * _sigmoid(c)
    o_ref[...] = jnp.dot(sc, w_ref[...], preferred_element_type=F32,
                         precision=lax.Precision.HIGHEST) + b_ref[...]


def _ada(c8, w_ada, b_ada):
    n = w_ada.shape[1]
    tn = 1024
    return pl.pallas_call(
        _ada_kernel,
        grid=(n // tn,),
        in_specs=[pl.BlockSpec((8, D_MODEL), lambda j: (0, 0)),
                  pl.BlockSpec((D_MODEL, tn), lambda j: (0, j)),
                  pl.BlockSpec((1, tn), lambda j: (0, j))],
        out_specs=pl.BlockSpec((8, tn), lambda j: (0, j)),
        out_shape=jax.ShapeDtypeStruct((8, n), F32),
        compiler_params=_cparams(("arbitrary",)),
        name="ada",
    )(c8, w_ada, b_ada.reshape(1, n))


def _chunk_cumsum(g, axis, reverse):
    n = g.shape[axis]
    pos = lax.broadcasted_iota(jnp.int32, g.shape, axis) % CHUNK
    acc = g
    s = 1
    while s < CHUNK:
        if reverse:
            sh = pltpu.roll(acc, n - s, axis)
            acc = acc + jnp.where(pos < CHUNK - s, sh, 0.0)
        else:
            sh = pltpu.roll(acc, s, axis)
            acc = acc + jnp.where(pos >= s, sh, 0.0)
        s *= 2
    return acc


def _in_kernel(x_ref, mod_ref, n1_ref, wmain_ref, wab_ref, alog_ref, dtb_ref,
               proj_ref, colg_ref, rowg_ref, h_scr):
    j = pl.program_id(1)

    @pl.when(j == 0)
    def _():
        x = x_ref[...]
        ms = jnp.mean(x * x, axis=-1, keepdims=True)
        y = x * lax.rsqrt(ms + EPS) * n1_ref[...]
        h = y * (1.0 + mod_ref[0, 1:2, :]) + mod_ref[0, 0:1, :]
        hb = h.astype(BF16)
        h_scr[...] = hb
        ab = _dot(hb, wab_ref[...])
        g = -jnp.exp(alog_ref[...]) * _softplus(ab + dtb_ref[...])
        beta = _sigmoid(ab)
        cum_f = _chunk_cumsum(g, 0, False)
        cum_b = _chunk_cumsum(g, 0, True)
        lane = lax.broadcasted_iota(jnp.int32, g.shape, 1)
        gc = jnp.where(lane % 16 < GDN_HEADS, cum_f, cum_b)
        gtot = cum_f + cum_b - g
        colg = jnp.where(lane < 16, gc, jnp.where(lane < 32, beta, jnp.where(lane < 48, gtot, 0.0)))
        colg_ref[...] = colg
        rowg_ref[...] = colg.T[0:16, :]

    proj_ref[...] = _dot(h_scr[...], wmain_ref[...]).astype(BF16)


def _stage_in(x2d, mod3, mod_base, t_len, n1, wmain, wab, alog, dtb, tm):
    m = x2d.shape[0]
    nt = m // tm
    nj = wmain.shape[1] // 1024
    tps = t_len // tm
    return pl.pallas_call(
        _in_kernel,
        grid=(nt, nj),
        in_specs=[pl.BlockSpec((tm, D_MODEL), lambda i, j: (i, 0)),
                  pl.BlockSpec((1, 6, D_MODEL), lambda i, j: (mod_base + i // tps, 0, 0)),
                  pl.BlockSpec((1, D_MODEL), lambda i, j: (0, 0)),
                  pl.BlockSpec((D_MODEL, 1024), lambda i, j: (0, j)),
                  pl.BlockSpec((D_MODEL, LANES), lambda i, j: (0, 0)),
                  pl.BlockSpec((1, LANES), lambda i, j: (0, 0)),
                  pl.BlockSpec((1, LANES), lambda i, j: (0, 0))],
        out_specs=[pl.BlockSpec((tm, 1024), lambda i, j: (i, j)),
                   pl.BlockSpec((tm, LANES), lambda i, j: (i, 0)),
                   pl.BlockSpec((16, tm), lambda i, j: (0, i))],
        out_shape=[jax.ShapeDtypeStruct((m, nj * 1024), BF16),
                   jax.ShapeDtypeStruct((m, LANES), F32),
                   jax.ShapeDtypeStruct((16, m), F32)],
        scratch_shapes=[pltpu.VMEM((tm, D_MODEL), BF16)],
        compiler_params=_cparams(("parallel", "arbitrary")),
        name="in_proj",
    )(x2d, mod3, n1, wmain, wab, alog, dtb)


def _conv_kernel(main_ref, prev_ref, next_ref, w_ref, out_ref, *, tc):
    t = pl.program_id(1)
    nt = pl.num_programs(1)
    part = pl.program_id(2)
    prev = jnp.where(t > 0, prev_ref[...].astype(F32), 0.0)
    nxt = jnp.where(t < nt - 1, next_ref[...].astype(F32), 0.0)
    xe = jnp.concatenate([prev, main_ref[...].astype(F32), nxt], axis=0)
    n = tc + 2 * HALO
    w = w_ref[...]
    acc = None
    for k in range(CONV_K):
        shift = (CONV_K // 2 - k) % n
        xs = xe if shift == 0 else pltpu.roll(xe, shift, 0)
        term = w[k:k + 1, :] * xs[HALO:HALO + tc, :]
        acc = term if acc is None else acc + term
    y = acc * _sigmoid(acc)
    outs = []
    for h in range(GDN_HEADS):
        yh = y[:, h * HEAD_DIM:(h + 1) * HEAD_DIM]
        ss = jnp.sum(yh * yh, axis=-1, keepdims=True)
        outs.append(yh * lax.rsqrt(ss + EPS))
    normed = jnp.concatenate(outs, axis=1)
    scale = jnp.where(part == 0, HEAD_DIM ** -0.5, 1.0)
    out_ref[...] = jnp.where(part < 2, normed * scale, y).astype(BF16)


def _stage_conv(proj, conv_w, nb, t_len, tc):
    m = proj.shape[0]
    nt = t_len // tc
    hb = tc // HALO
    last_halo = m // HALO - 1

    def main_map(b, t, p):
        return (b * nt + t, p)

    def prev_map(b, t, p):
        return (jnp.maximum((b * nt + t) * hb - 1, 0), p)

    def next_map(b, t, p):
        return (jnp.minimum((b * nt + t + 1) * hb, last_halo), p)

    return pl.pallas_call(
        functools.partial(_conv_kernel, tc=tc),
        grid=(nb, nt, 3),
        in_specs=[pl.BlockSpec((tc, GDN_WIDTH), main_map),
                  pl.BlockSpec((HALO, GDN_WIDTH), prev_map),
                  pl.BlockSpec((HALO, GDN_WIDTH), next_map),
                  pl.BlockSpec((CONV_K, GDN_WIDTH), lambda b, t, p: (0, p))],
        out_specs=pl.BlockSpec((tc, GDN_WIDTH), main_map),
        out_shape=jax.ShapeDtypeStruct((m, 3 * GDN_WIDTH), BF16),
        compiler_params=_cparams(("parallel", "parallel", "parallel")),
        name="conv_qkv",
    )(proj, proj, proj, conv_w)


def _tri_inv_all(a_list, ri, ci):
    b16 = (ri // 16) == (ci // 16)
    b32 = (ri // 32) == (ci // 32)
    off32 = jnp.logical_and(b32, jnp.logical_not(b16))
    eye = jnp.where(ri == ci, 1.0, 0.0)
    n16 = [jnp.where(b16, -a, 0.0) for a in a_list]
    t = [eye + n for n in n16]
    p = [n.astype(BF16) for n in n16]
    for _ in range(3):
        p = [_dot(x, x).astype(BF16) for x in p]
        t = [x + _dot(x.astype(BF16), y) for x, y in zip(t, p)]
    for keep in (off32, jnp.logical_not(b32)):
        tb = [x.astype(BF16) for x in t]
        mid = [_dot(x, jnp.where(keep, a, 0.0).astype(BF16)).astype(BF16) for x, a in zip(tb, a_list)]
        t = [x - _dot(y, z) for x, y, z in zip(t, mid, tb)]
    return t


def _gdn_kernel(qf, kf, vf, cgf, rgf, qb, kb, vb, cgb, rgb, of_ref, ob_ref, s_scr):
    hp = pl.program_id(1)

    @pl.when(pl.program_id(2) == 0)
    def _():
        s_scr[...] = jnp.zeros_like(s_scr)

    dirs = ((qf, kf, vf, cgf, rgf, of_ref), (qb, kb, vb, cgb, rgb, ob_ref))
    chains = [(d, hl) for d in range(2) for hl in range(GDN_HP)]
    n_c = SUB // CHUNK
    wide = 3 * LANES * GDN_HP
    rowi = lax.broadcasted_iota(jnp.int32, (LANES, wide), 0)
    coli = lax.broadcasted_iota(jnp.int32, (LANES, wide), 1)
    ri = lax.broadcasted_iota(jnp.int32, (SUB, SUB), 0)
    ci = lax.broadcasted_iota(jnp.int32, (SUB, SUB), 1)
    same = (ri // CHUNK) == (ci // CHUNK)
    incl = (jnp.logical_and(same, ri >= ci), jnp.logical_and(same, ri <= ci))

    bc = []
    for d in range(2):
        lane0 = d * GDN_HEADS + hp * GDN_HP
        target = lane0 + coli // (3 * LANES) + 16 * ((coli // LANES) % 3)
        sel = jnp.where(rowi == target, 1.0, 0.0).astype(BF16)
        c_hi, c_lo = _split(dirs[d][3][...])
        bc.append(_dot(c_hi, sel) + _dot(c_lo, sel))

    q, k, v, gcb, bb, gt, grow = {}, {}, {}, {}, {}, {}, {}
    for d, hl in chains:
        cols = slice(hl * HEAD_DIM, (hl + 1) * HEAD_DIM)
        q[d, hl] = dirs[d][0][:, cols].astype(F32)
        k[d, hl] = dirs[d][1][:, cols].astype(F32)
        v[d, hl] = dirs[d][2][:, cols].astype(F32)
        base = 3 * LANES * hl
        gcb[d, hl] = bc[d][:, base:base + LANES]
        bb[d, hl] = bc[d][:, base + LANES:base + 2 * LANES]
        gt[d, hl] = bc[d][:, base + 2 * LANES:base + 3 * LANES]
        grow[d, hl] = dirs[d][4][pl.ds(d * GDN_HEADS + hp * GDN_HP + hl, 1), :]

    eg = {c: jnp.exp(gcb[c]) for c in chains}
    kbeta = {c: k[c] * bb[c] for c in chains}
    rhs = {c: jnp.concatenate([v[c] * bb[c], kbeta[c] * eg[c]], axis=1).astype(BF16) for c in chains}
    qd = {c: q[c] * eg[c] for c in chains}
    kd = {c: (k[c] * jnp.exp(gt[c] - gcb[c])).astype(BF16) for c in chains}
    dec = {c: jnp.exp(jnp.where(incl[c[0]], jnp.concatenate([gcb[c], gcb[c]], axis=1) - grow[c], NEG))
           for c in chains}
    kq = {c: _dot_nt(jnp.concatenate([kbeta[c], q[c]], axis=0).astype(BF16), k[c].astype(BF16))
          for c in chains}
    a = [jnp.where(ri == ci, 0.0, kq[c][0:SUB] * dec[c]) for c in chains]
    attn = {c: (kq[c][SUB:2 * SUB] * dec[c]).astype(BF16) for c in chains}
    t = _tri_inv_all(a, ri, ci)
    uw = {c: _dot(x.astype(BF16), rhs[c]) for c, x in zip(chains, t)}

    s = {c: s_scr[c[0], c[1]] for c in chains}
    vns = {c: [None] * n_c for c in chains}
    o_s = {c: [None] * n_c for c in chains}
    for step in range(n_c):
        for c in chains:
            ck = step if c[0] == 0 else n_c - 1 - step
            lo, hi = ck * CHUNK, (ck + 1) * CHUNK
            lhs = jnp.concatenate([uw[c][lo:hi, LANES:2 * LANES], qd[c][lo:hi]], axis=0).astype(BF16)
            wq = _dot(lhs, s[c].astype(BF16))
            vn = (uw[c][lo:hi, 0:LANES] - wq[0:CHUNK]).astype(BF16)
            vns[c][ck] = vn
            o_s[c][ck] = wq[CHUNK:2 * CHUNK]
            s[c] = s[c] * jnp.exp(gt[c][lo:lo + 1, :]) + _dot_tn(kd[c][lo:hi], vn)
    for c in chains:
        s_scr[c[0], c[1]] = s[c]
        o = jnp.concatenate(o_s[c], axis=0) + _dot(attn[c], jnp.concatenate(vns[c], axis=0))
        dirs[c[0]][5][:, c[1] * HEAD_DIM:(c[1] + 1) * HEAD_DIM] = o.astype(BF16)


def _stage_gdn(qkvn, colg, rowg, nb, t_len):
    m = qkvn.shape[0]
    ns = t_len // SUB
    width = GDN_HP * HEAD_DIM
    per_part = GDN_HEADS // GDN_HP

    def fwd(b, h, t):
        return b * ns + t

    def bwd(b, h, t):
        return b * ns + (ns - 1 - t)

    def specs(blk):
        return [pl.BlockSpec((SUB, width), lambda b, h, t: (blk(b, h, t), h)),
                pl.BlockSpec((SUB, width), lambda b, h, t: (blk(b, h, t), per_part + h)),
                pl.BlockSpec((SUB, width), lambda b, h, t: (blk(b, h, t), 2 * per_part + h)),
                pl.BlockSpec((SUB, LANES), lambda b, h, t: (blk(b, h, t), 0)),
                pl.BlockSpec((16, SUB), lambda b, h, t: (0, blk(b, h, t)))]

    out_f = pl.BlockSpec((SUB, width), lambda b, h, t: (fwd(b, h, t), h))
    out_b = pl.BlockSpec((SUB, width), lambda b, h, t: (bwd(b, h, t), h))
    return pl.pallas_call(
        _gdn_kernel,
        grid=(nb, per_part, ns),
        in_specs=specs(fwd) + specs(bwd),
        out_specs=[out_f, out_b],
        out_shape=[jax.ShapeDtypeStruct((m, GDN_WIDTH), BF16),
                   jax.ShapeDtypeStruct((m, GDN_WIDTH), BF16)],
        scratch_shapes=[pltpu.VMEM((2, GDN_HP, HEAD_DIM, HEAD_DIM), F32)],
        compiler_params=_cparams(("parallel", "parallel", "arbitrary")),
        name="gdn",
    )(qkvn, qkvn, qkvn, colg, rowg, qkvn, qkvn, qkvn, colg, rowg)


def _out_kernel(of_ref, ob_ref, z_ref, p_ref, pp_ref, pn_ref, band_ref, x_ref, mod_ref,
                gnw_ref, pw_ref, ps_ref, woa_ref, wob_ref, n2_ref, x1_ref, h2t_ref,
                *, tm, tps, t_len):
    i = pl.program_id(0)
    tin = i % tps
    o = of_ref[...].astype(F32) + ob_ref[...].astype(F32)
    z = z_ref[...].astype(F32)
    gnw = gnw_ref[...]
    ya = []
    for h in range(GDN_HEADS):
        oh = o[:, h * HEAD_DIM:(h + 1) * HEAD_DIM]
        ms = jnp.mean(oh * oh, axis=-1, keepdims=True)
        ya.append(oh * lax.rsqrt(ms + EPS) * gnw)
    y_a = jnp.concatenate(ya, axis=1) * (z * _sigmoid(z))

    zero = jnp.zeros((HALO, POOL_WIDTH), BF16)
    prev = jnp.where(tin > 0, pp_ref[...], zero)
    nxt = jnp.where(tin < tps - 1, pn_ref[...], zero)
    p = p_ref[...]
    pe = jnp.concatenate([prev, p, nxt], axis=0)
    tpos = tin * tm + lax.broadcasted_iota(jnp.int32, (tm, POOL_GROUP), 0)
    yb = []
    for gi, win in enumerate(POOL_WINDOWS):
        half = win // 2
        cols = slice(gi * POOL_GROUP, (gi + 1) * POOL_GROUP)
        wsum = _dot(band_ref[gi], pe[:, cols])
        cnt = (jnp.minimum(tpos + half, t_len) - jnp.maximum(tpos - half, 0)).astype(F32)
        pooled = wsum / cnt - p[:, cols].astype(F32)
        yb.append(_dot(pooled.astype(BF16), pw_ref[gi]))
    y_b = jnp.concatenate(yb, axis=1) * ps_ref[...]

    mixed = _dot(y_a.astype(BF16), woa_ref[...]) + _dot(y_b.astype(BF16), wob_ref[...])
    x1 = x_ref[...] + mod_ref[0, 2:3, :] * mixed
    x1_ref[...] = x1
    ms = jnp.mean(x1 * x1, axis=-1, keepdims=True)
    h2 = x1 * lax.rsqrt(ms + EPS) * n2_ref[...]
    h2 = h2 * (1.0 + mod_ref[0, 4:5, :]) + mod_ref[0, 3:4, :]
    h2t_ref[...] = h2.T.astype(BF16)


def _pool_bands(tm):
    t = jnp.arange(tm)[:, None]
    s = jnp.arange(tm + 2 * HALO)[None, :] - HALO
    bands = []
    for win in POOL_WINDOWS:
        half = win // 2
        bands.append(jnp.logical_and(s >= t - half, s <= t + half - 1))
    return jnp.stack(bands).astype(BF16)


def _stage_out(o_f, o_b, proj, x2d, mod3, mod_base, t_len, gnw, pool_w, pool_scale,
               wo_a, wo_b, n2, tm):
    m = x2d.shape[0]
    nt = m // tm
    tps = t_len // tm
    hb = tm // HALO
    last_halo = m // HALO - 1
    pcol = 4 * GDN_WIDTH // POOL_WIDTH
    band = _pool_bands(tm)
    const = lambda i: (0, 0)
    return pl.pallas_call(
        functools.partial(_out_kernel, tm=tm, tps=tps, t_len=t_len),
        grid=(nt,),
        in_specs=[pl.BlockSpec((tm, GDN_WIDTH), lambda i: (i, 0)),
                  pl.BlockSpec((tm, GDN_WIDTH), lambda i: (i, 0)),
                  pl.BlockSpec((tm, GDN_WIDTH), lambda i: (i, 3)),
                  pl.BlockSpec((tm, POOL_WIDTH), lambda i: (i, pcol)),
                  pl.BlockSpec((HALO, POOL_WIDTH), lambda i: (jnp.maximum(i * hb - 1, 0), pcol)),
                  pl.BlockSpec((HALO, POOL_WIDTH), lambda i: (jnp.minimum((i + 1) * hb, last_halo), pcol)),
                  pl.BlockSpec((4, tm, tm + 2 * HALO), lambda i: (0, 0, 0)),
                  pl.BlockSpec((tm, D_MODEL), lambda i: (i, 0)),
                  pl.BlockSpec((1, 6, D_MODEL), lambda i: (mod_base + i // tps, 0, 0)),
                  pl.BlockSpec((1, HEAD_DIM), const),
                  pl.BlockSpec((4, POOL_GROUP, POOL_GROUP), lambda i: (0, 0, 0)),
                  pl.BlockSpec((1, POOL_WIDTH), const),
                  pl.BlockSpec((GDN_WIDTH, D_MODEL), const),
                  pl.BlockSpec((POOL_WIDTH, D_MODEL), const),
                  pl.BlockSpec((1, D_MODEL), const)],
        out_specs=[pl.BlockSpec((tm, D_MODEL), lambda i: (i, 0)),
                   pl.BlockSpec((D_MODEL, tm), lambda i: (0, i))],
        out_shape=[jax.ShapeDtypeStruct((m, D_MODEL), F32),
                   jax.ShapeDtypeStruct((D_MODEL, m), BF16)],
        compiler_params=_cparams(("parallel",)),
        name="mix_out",
    )(o_f, o_b, proj, proj, proj, proj, band, x2d, mod3, gnw, pool_w, pool_scale, wo_a, wo_b, n2)


def _top_rows(cur, n, scr, want_rank=False):
    rows, tm = cur.shape
    cur = cur.reshape(rows // 8, 8, tm)
    rank = jnp.full(cur.shape, float(n), F32) if want_rank else None
    for r in range(n):
        m8 = jnp.max(cur, axis=0)
        for s in (1, 2, 4):
            m8 = jnp.maximum(m8, pltpu.roll(m8, s, 0))
        scr[r:r + 1, :] = m8[0:1, :]
        hit = cur >= m8[None]
        if want_rank:
            rank = jnp.where(hit, float(r), rank)
        cur = jnp.where(hit, NEG, cur)
    return rank.reshape(rows, tm) if want_rank else None


def _route_kernel(h2t_ref, wqt_ref, khi_ref, klo_ref, lim_ref, r2_ref, w_ref, e2_ref,
                  qt_scr, a_scr, b_scr, c_scr):
    qt_scr[...] = _dot(wqt_ref[...], h2t_ref[...])
    for h in range(PEER_HEADS):
        sc = []
        for p in range(2):
            hp = 2 * h + p
            qh, ql = _split(qt_scr[hp * N_KEYS:(hp + 1) * N_KEYS, :])
            sc.append(_dot(khi_ref[hp], qh) + _dot(khi_ref[hp], ql) + _dot(klo_ref[hp], qh))
        s1, s2 = sc
        rank1 = _top_rows(s1, PEER_TOPK, a_scr, want_rank=True)
        rank2 = _top_rows(s2, PEER_TOPK, b_scr, want_rank=True)
        a = a_scr[...]
        b = b_scr[...]
        cands = [a[0:1, :] + b[0:8, :], a[0:1, :] + b[8:16, :]]
        for p in range(1, 8):
            cands.append(a[p:p + 1, :] + b[0:8, :])
        cands.append(a[8:16, :] + b[0:1, :])
        _top_rows(jnp.concatenate(cands, axis=0), PEER_TOPK + 1, c_scr)
        c = c_scr[0:PEER_TOPK + 1, :]
        zsum = jnp.sum(jnp.exp(c[0:PEER_TOPK, :] - c[0:1, :]), axis=0, keepdims=True)
        tau = 0.5 * (c[PEER_TOPK - 1:PEER_TOPK, :] + c[PEER_TOPK:PEER_TOPK + 1, :])
        count = jnp.zeros_like(a)
        for qi in range(PEER_TOPK):
            count = count + jnp.where(a + b[qi:qi + 1, :] >= tau, 1.0, 0.0)
        lim = jnp.zeros_like(s1)
        for p in range(PEER_TOPK):
            lim = jnp.where(rank1 == float(p), count[p:p + 1, :], lim)
        lim_ref[h] = lim
        r2_ref[h] = rank2.astype(BF16)
        w_ref[h] = 0.5 * jnp.exp(s1 - a[0:1, :]) / zsum
        e2_ref[h] = jnp.exp(s2 - b[0:1, :]).astype(BF16)


def _stage_route(h2t, wqt, k_hi, k_lo, tm):
    m = h2t.shape[1]
    nt = m // tm
    rspec = pl.BlockSpec((PEER_HEADS, N_KEYS, tm), lambda i: (0, 0, i))
    row_shape = jax.ShapeDtypeStruct((PEER_HEADS, N_KEYS, m), F32)
    col_shape = jax.ShapeDtypeStruct((PEER_HEADS, N_KEYS, m), BF16)
    return pl.pallas_call(
        _route_kernel,
        grid=(nt,),
        in_specs=[pl.BlockSpec((D_MODEL, tm), lambda i: (0, i)),
                  pl.BlockSpec((D_MODEL, D_MODEL), lambda i: (0, 0)),
                  pl.BlockSpec((2 * PEER_HEADS, N_KEYS, N_KEYS), lambda i: (0, 0, 0)),
                  pl.BlockSpec((2 * PEER_HEADS, N_KEYS, N_KEYS), lambda i: (0, 0, 0))],
        out_specs=[rspec, rspec, rspec, rspec],
        out_shape=[row_shape, col_shape, row_shape, col_shape],
        scratch_shapes=[pltpu.VMEM((D_MODEL, tm), F32),
                        pltpu.VMEM((PEER_TOPK, tm), F32),
                        pltpu.VMEM((PEER_TOPK, tm), F32),
                        pltpu.VMEM((24, tm), F32)],
        compiler_params=_cparams(("parallel",)),
        name="peer_route",
    )(h2t, wqt, k_hi, k_lo)


def _peer_kernel(h2t_ref, u_ref, vt_ref, lim_ref, r2_ref, w_ref, e2_ref, x1_ref, mod_ref,
                 nf_ref, y_ref, acc_scr, hu_scr, act_scr, bl_scr, bw_scr):
    e = pl.program_id(1)
    tm = h2t_ref.shape[1]
    n_sub = PEER_TE // N_KEYS
    n_half = n_sub // 2
    he = PEER_TE // 2
    pack = 16
    k_piece = D_MODEL // n_half
    o_piece = D_MODEL // n_half
    halves = (slice(0, he), slice(he, PEER_TE))

    @pl.when(e == 0)
    def _():
        acc_scr[...] = jnp.zeros_like(acc_scr)

    def gate_act(ii):
        row = (e % (8 // n_sub)) * n_sub + ii
        for h in range(PEER_HEADS):
            bl_scr[h] = jnp.broadcast_to(lim_ref[h, pl.ds(row, 1), :], (pack, tm)).astype(BF16)
            bw_scr[h] = jnp.broadcast_to(w_ref[h, pl.ds(row, 1), :], (pack, tm)).astype(BF16)
        for rc in range(N_KEYS // PEER_RC):
            jr = slice(rc * PEER_RC, (rc + 1) * PEER_RC)
            er = slice(ii * N_KEYS + rc * PEER_RC, ii * N_KEYS + (rc + 1) * PEER_RC)
            gate = None
            for h in range(PEER_HEADS):
                r2c = r2_ref[h, jr, :].reshape(PEER_RC // pack, pack, tm)
                e2c = e2_ref[h, jr, :].reshape(PEER_RC // pack, pack, tm)
                term = jnp.where(r2c < bl_scr[h][None], e2c * bw_scr[h][None], jnp.zeros_like(e2c))
                gate = term if gate is None else gate + term
            x = hu_scr[er, :]
            y = (x * (1.0 + lax.erf(x * (2.0 ** -0.5)))).astype(BF16)
            act_scr[er, :] = y * gate.reshape(PEER_RC, tm)

    hu_scr[halves[0], :] = _dot(u_ref[halves[0], :], h2t_ref[...])
    for ii in range(n_half):
        gate_act(ii)
        kk = slice(ii * k_piece, (ii + 1) * k_piece)
        part = _dot(u_ref[halves[1], kk], h2t_ref[kk, :])
        if ii == 0:
            hu_scr[halves[1], :] = part
        else:
            hu_scr[halves[1], :] += part
    for ii in range(n_half):
        gate_act(n_half + ii)
        rr = slice(ii * o_piece, (ii + 1) * o_piece)
        acc_scr[rr, :] += _dot(vt_ref[rr, halves[0]], act_scr[halves[0], :])
    acc_scr[...] += _dot(vt_ref[:, halves[1]], act_scr[halves[1], :])

    @pl.when(e == pl.num_programs(1) - 1)
    def _():
        xo = x1_ref[...] + mod_ref[0, 5:6, :] * acc_scr[...].T
        ms = jnp.mean(xo * xo, axis=-1, keepdims=True)
        y_ref[...] = xo * lax.rsqrt(ms + EPS) * nf_ref[...]


def _stage_peer(h2t, u_bf, vt_bf, route, x1, mod3, mod_base, t_len, nf, tm):
    m = x1.shape[0]
    nt = m // tm
    ne = u_bf.shape[0] // PEER_TE
    tps = t_len // tm
    per_blk = 8 * N_KEYS // PEER_TE
    row_spec = pl.BlockSpec((PEER_HEADS, 8, tm), lambda i, e: (0, e // per_blk, i))
    col_spec = pl.BlockSpec((PEER_HEADS, N_KEYS, tm), lambda i, e: (0, 0, i))
    return pl.pallas_call(
        _peer_kernel,
        grid=(nt, ne),
        in_specs=[pl.BlockSpec((D_MODEL, tm), lambda i, e: (0, i)),
                  pl.BlockSpec((PEER_TE, D_MODEL), lambda i, e: (e, 0)),
                  pl.BlockSpec((D_MODEL, PEER_TE), lambda i, e: (0, e)),
                  row_spec, col_spec, row_spec, col_spec,
                  pl.BlockSpec((tm, D_MODEL), lambda i, e: (i, 0)),
                  pl.BlockSpec((1, 6, D_MODEL), lambda i, e: (mod_base + i // tps, 0, 0)),
                  pl.BlockSpec((1, D_MODEL), lambda i, e: (0, 0))],
        out_specs=pl.BlockSpec((tm, D_MODEL), lambda i, e: (i, 0)),
        out_shape=jax.ShapeDtypeStruct((m, D_MODEL), F32),
        scratch_shapes=[pltpu.VMEM((D_MODEL, tm), F32),
                        pltpu.VMEM((PEER_TE, tm), F32),
                        pltpu.VMEM((PEER_TE, tm), BF16),
                        pltpu.VMEM((PEER_HEADS, 16, tm), BF16),
                        pltpu.VMEM((PEER_HEADS, 16, tm), BF16)],
        compiler_params=_cparams(("parallel", "arbitrary")),
        name="peer_dense",
    )(h2t, u_bf, vt_bf, *route, x1, mod3, nf)


def _prepare_weights(norm1_w, w_in, conv_w, a_log, dt_bias, gdn_norm_w, pool_w, pool_scale,
                     w_out, norm2_w, peer_wq, peer_keys, expert_u, expert_v, norm_f_w):
    o0 = 3 * GDN_WIDTH
    o1 = o0 + GDN_WIDTH
    o2 = o1 + 2 * GDN_HEADS
    o3 = o2 + 2 * GDN_HEADS
    w_in = w_in[0]
    wmain = jnp.concatenate([w_in[:, :o1], w_in[:, o3:]], axis=1).astype(BF16)
    wa, wb = w_in[:, o1:o2], w_in[:, o2:o3]
    pad = jnp.zeros((D_MODEL, LANES - 48), F32)
    wab = jnp.concatenate([wa, wb, wa, pad], axis=1).astype(BF16)
    z16 = jnp.zeros((16,), F32)
    z80 = jnp.zeros((LANES - 48,), F32)
    al = a_log[0].reshape(16)
    dtb = dt_bias[0].reshape(16)
    alog = jnp.concatenate([al, z16, al, z80]).reshape(1, LANES)
    dtbias = jnp.concatenate([dtb, z16, dtb, z80]).reshape(1, LANES)
    keys = peer_keys[0].reshape(2 * PEER_HEADS, N_KEYS, N_KEYS)
    k_hi = keys.astype(BF16)
    k_lo = (keys - k_hi.astype(F32)).astype(BF16)
    return dict(
        n1=norm1_w[0].reshape(1, D_MODEL), wmain=wmain, wab=wab, alog=alog, dtb=dtbias,
        conv_w=conv_w[0], gnw=gdn_norm_w[0].reshape(1, HEAD_DIM),
        pool_w=pool_w[0].astype(BF16), pool_scale=pool_scale[0].reshape(1, POOL_WIDTH),
        wo_a=w_out[0][:GDN_WIDTH].astype(BF16), wo_b=w_out[0][GDN_WIDTH:].astype(BF16),
        n2=norm2_w[0].reshape(1, D_MODEL), wqt=peer_wq[0].astype(BF16).T, k_hi=k_hi, k_lo=k_lo,
        u=expert_u[0].astype(BF16), vt=expert_v[0].astype(BF16).T,
        nf=norm_f_w.reshape(1, D_MODEL))


def _trunk_group(x, mod3, mod_base, wts):
    nb, t_len, _ = x.shape
    m = nb * t_len
    x2d = x.reshape(m, D_MODEL)
    tm = TOKEN_TILE
    proj, colg, rowg = _stage_in(x2d, mod3, mod_base, t_len, wts["n1"], wts["wmain"],
                                 wts["wab"], wts["alog"], wts["dtb"], tm)
    qkvn = _stage_conv(proj, wts["conv_w"], nb, t_len, tm)
    o_f, o_b = _stage_gdn(qkvn, colg, rowg, nb, t_len)
    x1, h2t = _stage_out(o_f, o_b, proj, x2d, mod3, mod_base, t_len, wts["gnw"], wts["pool_w"],
                         wts["pool_scale"], wts["wo_a"], wts["wo_b"], wts["n2"], tm)
    route = _stage_route(h2t, wts["wqt"], wts["k_hi"], wts["k_lo"], tm)
    y = _stage_peer(h2t, wts["u"], wts["vt"], route, x1, mod3, mod_base, t_len, wts["nf"], tm)
    return y.reshape(nb, t_len, D_MODEL)


def kernel(x_prompt, x_sample, c_prompt, c_sample, w_ada, b_ada, norm1_w, w_in, conv_w, a_log,
           dt_bias, gdn_norm_w, pool_w, pool_scale, w_out, norm2_w, peer_wq, peer_keys,
           expert_u, expert_v, norm_f_w):
    nbp = c_prompt.shape[0]
    nbs = c_sample.shape[0]
    c8 = jnp.concatenate([c_prompt, c_sample, jnp.zeros((8 - nbp - nbs, D_MODEL), F32)], axis=0)
    mod3 = _ada(c8, w_ada[0], b_ada[0]).reshape(8, 6, D_MODEL)
    wts = _prepare_weights(norm1_w, w_in, conv_w, a_log, dt_bias, gdn_norm_w, pool_w, pool_scale,
                           w_out, norm2_w, peer_wq, peer_keys, expert_u, expert_v, norm_f_w)
    y_prompt = _trunk_group(x_prompt, mod3, 0, wts)
    y_sample = _trunk_group(x_sample, mod3, nbp, wts)
    return (y_prompt, y_sample)
```
